```python
import jax, jax.numpy as jnp
from jax import lax
import numpy as np

D_MODEL = 1024
BATCH = 8
SEQ = 2048
DEPTH = 4

N_MIXERS = 2
A_HEADS = 8
A_HEAD_DIM = D_MODEL // A_HEADS
MOBA_BLOCK = 256
MOBA_TOPK = 3
MOBA_QCHUNK = 64
R_HEADS = 4
R_QK_DIM = D_MODEL // R_HEADS
R_V_DIM = 2 * D_MODEL // R_HEADS
R_CHUNK = 128
D_FF = 4 * D_MODEL
ROPE_THETA = 10000.0
EPS = 1e-6
NEG = -1e30
N_A = (DEPTH + 1) // 2
N_R = DEPTH // 2

kernel_name = "moba_retention_hybrid_trunk"


def rms_norm(x, g):
    xf = x.astype(jnp.float32)
    y = xf * lax.rsqrt(jnp.mean(xf * xf, axis=-1, keepdims=True) + EPS)
    return (y * g.astype(jnp.float32)).astype(x.dtype)


def rope(x, pos):
    half = x.shape[-1] // 2
    inv = ROPE_THETA ** (-jnp.arange(half, dtype=jnp.float32) / half)
    ang = pos.astype(jnp.float32)[:, None] * inv[None, :]
    cos = jnp.cos(ang)[:, None, :]
    sin = jnp.sin(ang)[:, None, :]
    xf = x.astype(jnp.float32)
    x1, x2 = xf[..., :half], xf[..., half:]
    return jnp.concatenate([x1 * cos - x2 * sin, x2 * cos + x1 * sin], axis=-1).astype(x.dtype)


def moba_attention(q, k, v):
    B, H, S, Dh = q.shape
    nb = -(-S // MOBA_BLOCK)
    pad = nb * MOBA_BLOCK - S
    kp = jnp.pad(k, ((0, 0), (0, 0), (0, pad), (0, 0)))
    vp = jnp.pad(v, ((0, 0), (0, 0), (0, pad), (0, 0)))
    kb = kp.reshape(B, H, nb, MOBA_BLOCK, Dh)
    vb = vp.reshape(B, H, nb, MOBA_BLOCK, Dh)
    kmean = jnp.mean(kb.astype(jnp.float32), axis=3)
    ksel = min(MOBA_TOPK, nb)
    scale = Dh ** -0.5
    n_chunks = S // MOBA_QCHUNK
    qc = q.reshape(B, H, n_chunks, MOBA_QCHUNK, Dh).transpose(2, 0, 1, 3, 4)
    bi = jnp.arange(B)[:, None, None]
    hi = jnp.arange(H)[None, :, None]

    def one_chunk(args):
        c, q_c = args
        start = c * MOBA_QCHUNK
        own = start // MOBA_BLOCK
        qpos = start + jnp.arange(MOBA_QCHUNK)
        gate = jnp.einsum('bhqd,bhnd->bhqn', q_c, kmean, preferred_element_type=jnp.float32)
        gate = jnp.where(jnp.arange(nb) < own, gate, NEG)
        _, top_i = lax.top_k(gate, ksel)
        valid = top_i < own
        k_own = lax.dynamic_slice_in_dim(kp, own * MOBA_BLOCK, MOBA_BLOCK, axis=2)
        v_own = lax.dynamic_slice_in_dim(vp, own * MOBA_BLOCK, MOBA_BLOCK, axis=2)
        kpos = own * MOBA_BLOCK + jnp.arange(MOBA_BLOCK)
        s_own = jnp.einsum('bhqd,bhkd->bhqk', q_c, k_own, preferred_element_type=jnp.float32) * scale
        s_own = jnp.where(kpos[None, :] <= qpos[:, None], s_own, NEG)
        scores = [s_own]
        for j in range(ksel):
            k_sel = kb[bi, hi, top_i[..., j]]
            s = jnp.einsum('bhqd,bhqkd->bhqk', q_c, k_sel, preferred_element_type=jnp.float32) * scale
            scores.append(jnp.where(valid[..., j, None], s, NEG))
        p = jax.nn.softmax(jnp.concatenate(scores, axis=-1), axis=-1).astype(v.dtype)
        p_parts = jnp.split(p, 1 + ksel, axis=-1)
        out = jnp.einsum('bhqk,bhkd->bhqd', p_parts[0], v_own)
        for j in range(ksel):
            v_sel = vb[bi, hi, top_i[..., j]]
            out = out + jnp.einsum('bhqk,bhqkd->bhqd', p_parts[j + 1], v_sel)
        return out

    out = lax.map(one_chunk, (jnp.arange(n_chunks), qc))
    return out.transpose(1, 2, 0, 3, 4).reshape(B, H, S, Dh)


def moba_mixer(h, w_qkv, q_gain, k_gain, w_o, pos):
    B, S, _ = h.shape
    q, k, v = jnp.split(h @ w_qkv, 3, axis=-1)
    q = rope(rms_norm(q.reshape(B, S, A_HEADS, A_HEAD_DIM), q_gain), pos)
    k = rope(rms_norm(k.reshape(B, S, A_HEADS, A_HEAD_DIM), k_gain), pos)
    v = v.reshape(B, S, A_HEADS, A_HEAD_DIM)
    o = moba_attention(q.transpose(0, 2, 1, 3), k.transpose(0, 2, 1, 3), v.transpose(0, 2, 1, 3))
    return o.transpose(0, 2, 1, 3).reshape(B, S, D_MODEL) @ w_o


def retention_chunkwise(q, k, v):
    B, H, S, Dk = q.shape
    Dv = v.shape[-1]
    C = R_CHUNK
    nc = S // C
    log_g = jnp.log1p(-(2.0 ** (-5.0 - jnp.arange(H, dtype=jnp.float32))))
    idx = jnp.arange(C, dtype=jnp.float32)
    diff = idx[:, None] - idx[None, :]
    dmat = jnp.where(diff >= 0, jnp.exp(log_g[:, None, None] * jnp.maximum(diff, 0.0)), 0.0)
    xi = jnp.exp(log_g[:, None] * (idx + 1.0))
    zeta = jnp.exp(log_g[:, None] * (C - 1.0 - idx))
    g_chunk = jnp.exp(log_g * C)
    qc = q.astype(jnp.float32).reshape(B, H, nc, C, Dk)
    kc = k.astype(jnp.float32).reshape(B, H, nc, C, Dk)
    vc = v.astype(jnp.float32).reshape(B, H, nc, C, Dv)
    s = jnp.einsum('bhnqd,bhnkd->bhnqk', qc, kc) * dmat[None, :, None]
    inner = jnp.einsum('bhnqk,bhnkd->bhnqd', s, vc)

    def step(R, inp):
        q_i, k_i, v_i = inp
        cross = jnp.einsum('bhqd,bhde->bhqe', q_i, R) * xi[None, :, :, None]
        R = R * g_chunk[None, :, None, None] + jnp.einsum('bhkd,bhke->bhde', k_i * zeta[None, :, :, None], v_i)
        return R, cross

    R0 = jnp.zeros((B, H, Dk, Dv), jnp.float32)
    _, cross = lax.scan(step, R0, (jnp.moveaxis(qc, 2, 0), jnp.moveaxis(kc, 2, 0), jnp.moveaxis(vc, 2, 0)))
    out = inner + cross.transpose(1, 2, 0, 3, 4)
    return out.reshape(B, H, S, Dv)


def retention_mixer(h, w_in, w_out, pos):
    B, S, _ = h.shape
    q, k, v, g = jnp.split(h @ w_in, [D_MODEL, 2 * D_MODEL, 4 * D_MODEL], axis=-1)
    q = rope(q.reshape(B, S, R_HEADS, R_QK_DIM), pos)
    k = rope(k.reshape(B, S, R_HEADS, R_QK_DIM), pos) * (R_QK_DIM ** -0.5)
    v = v.reshape(B, S, R_HEADS, R_V_DIM)
    y = retention_chunkwise(q.transpose(0, 2, 1, 3), k.transpose(0, 2, 1, 3), v.transpose(0, 2, 1, 3))
    y = y * lax.rsqrt(jnp.mean(y * y, axis=-1, keepdims=True) + EPS)
    y = y.transpose(0, 2, 1, 3).reshape(B, S, 2 * D_MODEL).astype(h.dtype)
    return (jax.nn.silu(g) * y) @ w_out


def sqrelu_mlp(h, w1, w2):
    return jnp.square(jax.nn.relu(h @ w1)) @ w2


def setup_inputs(seed: int = 0) -> dict:
    key = jax.random.key(seed)
    ks = jax.random.split(key, 12)
    f32 = jnp.float32
    D = D_MODEL
    return {
        "x": jax.random.normal(ks[0], (BATCH, SEQ, D), f32),
        "norm_mix_g": 1.0 + 0.05 * jax.random.normal(ks[1], (DEPTH, D), f32),
        "norm_mlp_g": 1.0 + 0.05 * jax.random.normal(ks[2], (DEPTH, D), f32),
        "a_w_qkv": jax.random.normal(ks[3], (N_A, D, 3 * D), f32) * D ** -0.5,
        "a_q_gain": 1.0 + 0.05 * jax.random.normal(ks[4], (N_A, A_HEAD_DIM), f32),
        "a_k_gain": 1.0 + 0.05 * jax.random.normal(ks[5], (N_A, A_HEAD_DIM), f32),
        "a_w_o": jax.random.normal(ks[6], (N_A, D, D), f32) * D ** -0.5,
        "r_w_in": jax.random.normal(ks[7], (N_R, D, 6 * D), f32) * D ** -0.5,
        "r_w_out": jax.random.normal(ks[8], (N_R, 2 * D, D), f32) * (2 * D) ** -0.5,
        "mlp_w1": jax.random.normal(ks[9], (DEPTH, D, D_FF), f32) * D ** -0.5,
        "mlp_w2": jax.random.normal(ks[10], (DEPTH, D_FF, D), f32) * D_FF ** -0.5,
    }


def reference(x, norm_mix_g, norm_mlp_g, a_w_qkv, a_q_gain, a_k_gain, a_w_o, r_w_in, r_w_out, mlp_w1, mlp_w2):
    pos = jnp.arange(x.shape[1], dtype=jnp.int32)
    for i in range(DEPTH):
        h = rms_norm(x, norm_mix_g[i])
        if i % N_MIXERS == 0:
            j = i // N_MIXERS
            x = x + moba_mixer(h, a_w_qkv[j], a_q_gain[j], a_k_gain[j], a_w_o[j], pos)
        else:
            j = i // N_MIXERS
            x = x + retention_mixer(h, r_w_in[j], r_w_out[j], pos)
        x = x + sqrelu_mlp(rms_norm(x, norm_mlp_g[i]), mlp_w1[i], mlp_w2[i])
    return x
```

```python
import functools

import jax
import jax.numpy as jnp
import numpy as np
from jax import lax
from jax.experimental import pallas as pl
from jax.experimental.pallas import tpu as pltpu

D_MODEL = 1024
A_HEADS = 8
A_HEAD_DIM = D_MODEL // A_HEADS
MOBA_BLOCK = 256
MOBA_TOPK = 3
R_HEADS = 4
R_QK_DIM = D_MODEL // R_HEADS
R_V_DIM = 2 * D_MODEL // R_HEADS
R_CHUNK = 128
D_FF = 4 * D_MODEL
ROPE_THETA = 10000.0
EPS = 1e-6
NEG = -1e30

LANES = 128
ROW_TILE = 512
VMEM_LIMIT = 56 * 1024 * 1024

BF16 = jnp.bfloat16
F32 = jnp.float32


def _dot(a, b):
    return jnp.dot(a, b, preferred_element_type=F32)


def _dot_nt(a, b):
    return lax.dot_general(a, b, (((1,), (1,)), ((), ())), preferred_element_type=F32)


def _dot_tn(a, b):
    return lax.dot_general(a, b, (((0,), (0,)), ((), ())), preferred_element_type=F32)


def _rms(x, g):
    return x * lax.rsqrt(jnp.mean(x * x, axis=-1, keepdims=True) + EPS) * g


def _params():
    return pltpu.CompilerParams(dimension_semantics=("arbitrary",), vmem_limit_bytes=VMEM_LIMIT)


def _resident(shape):
    nd = len(shape)
    return pl.BlockSpec(shape, lambda *_: (0,) * nd, pipeline_mode=pl.Buffered(1))


def _moba_qkv_kernel(x_ref, g_ref, w_ref, qg_ref, kg_ref, cos_ref, sin_ref, q_ref, k_ref, v_ref):
    h = _rms(x_ref[...], g_ref[...]).astype(BF16)
    cos = cos_ref[...]
    sin = sin_ref[...]
    scale = A_HEAD_DIM ** -0.5

    def qk_norm_rope(y, gain):
        y = _rms(y, gain)
        return y * cos + pltpu.roll(y, A_HEAD_DIM // 2, axis=1) * sin

    yq = _dot(h, w_ref[:, 0:D_MODEL])
    for hh in range(A_HEADS):
        sl = slice(hh * A_HEAD_DIM, (hh + 1) * A_HEAD_DIM)
        q_ref[:, sl] = (qk_norm_rope(yq[:, sl], qg_ref[...]) * scale).astype(BF16)
    yk = _dot(h, w_ref[:, D_MODEL:2 * D_MODEL])
    for hh in range(A_HEADS):
        sl = slice(hh * A_HEAD_DIM, (hh + 1) * A_HEAD_DIM)
        k_ref[:, sl] = qk_norm_rope(yk[:, sl], kg_ref[...]).astype(BF16)
    v_ref[...] = _dot(h, w_ref[:, 2 * D_MODEL:3 * D_MODEL]).astype(BF16)


def _moba_qkv(x2, g, w, qg, kg, cos, sin, seq):
    t = x2.shape[0]
    tiles_per_seq = seq // ROW_TILE
    row = lambda i: (i, 0)
    pos = lambda i: (i % tiles_per_seq, 0)
    out = jax.ShapeDtypeStruct((t, D_MODEL), BF16)
    return pl.pallas_call(
        _moba_qkv_kernel,
        grid=(t // ROW_TILE,),
        in_specs=[
            pl.BlockSpec((ROW_TILE, D_MODEL), row),
            _resident((1, D_MODEL)),
            _resident((D_MODEL, 3 * D_MODEL)),
            _resident((1, A_HEAD_DIM)),
            _resident((1, A_HEAD_DIM)),
            pl.BlockSpec((ROW_TILE, A_HEAD_DIM), pos),
            pl.BlockSpec((ROW_TILE, A_HEAD_DIM), pos),
        ],
        out_specs=[pl.BlockSpec((ROW_TILE, D_MODEL), row)] * 3,
        out_shape=[out, out, out],
        compiler_params=_params(),
        name="moba_qkv",
    )(x2, g, w, qg, kg, cos, sin)


def _moba_attn_kernel(q_ref, k_ref, v_ref, o_ref, *, seq):
    nb = seq // MOBA_BLOCK
    k = k_ref[...]
    v = v_ref[...]
    kmean = jnp.mean(k.astype(F32).reshape(nb, MOBA_BLOCK, A_HEAD_DIM), axis=1)
    kb = lax.broadcasted_iota(jnp.int32, (seq, LANES), 0) // MOBA_BLOCK
    kc = lax.broadcasted_iota(jnp.int32, (seq, LANES), 1)
    k_aug = jnp.concatenate([k, (kb == kc).astype(BF16)], axis=1)
    v_aug = jnp.concatenate([v, jnp.ones((seq, LANES), BF16)], axis=1)
    blk = lax.broadcasted_iota(jnp.int32, (nb, MOBA_BLOCK), 0)
    qi_idx = lax.broadcasted_iota(jnp.int32, (MOBA_BLOCK, MOBA_BLOCK), 0)
    ki_idx = lax.broadcasted_iota(jnp.int32, (MOBA_BLOCK, MOBA_BLOCK), 1)
    causal = ki_idx <= qi_idx

    for i in range(nb):
        rows = slice(i * MOBA_BLOCK, (i + 1) * MOBA_BLOCK)
        q = q_ref[rows, :]
        gate = lax.dot_general(kmean, q.astype(F32), (((1,), (1,)), ((), ())),
                               preferred_element_type=F32, precision=lax.Precision.HIGHEST)
        gate = jnp.where(blk < i, gate, NEG)
        rank = jnp.zeros((nb, MOBA_BLOCK), jnp.int32)
        for j in range(nb):
            gj = gate[j:j + 1, :]
            beats = (gj > gate) | ((gj == gate) & (blk > j))
            rank = rank + beats.astype(jnp.int32)
        chosen = ((rank < MOBA_TOPK) & (blk < i)) | (blk == i)
        bias_t = jnp.where(chosen, 0.0, NEG).astype(F32)
        bias_t = jnp.concatenate([bias_t, jnp.zeros((LANES - nb, MOBA_BLOCK), F32)], axis=0)
        bias = bias_t.T.astype(BF16)
        q_aug = jnp.concatenate([q, bias], axis=1)
        ncols = (i + 1) * MOBA_BLOCK
        s = _dot_nt(q_aug, k_aug[:ncols, :])
        s_diag = jnp.where(causal, s[:, i * MOBA_BLOCK:], NEG)
        s = s_diag if i == 0 else jnp.concatenate([s[:, :i * MOBA_BLOCK], s_diag], axis=1)
        m = jnp.max(s, axis=-1, keepdims=True)
        p = jnp.exp(s - m).astype(BF16)
        o = _dot(p, v_aug[:ncols, :])
        o_ref[rows, :] = (o[:, :A_HEAD_DIM] / o[:, A_HEAD_DIM:A_HEAD_DIM + 1]).astype(BF16)


def _moba_attn(q, k, v, batch, seq):
    spec = pl.BlockSpec((seq, A_HEAD_DIM), lambda b, h: (b, h))
    return pl.pallas_call(
        functools.partial(_moba_attn_kernel, seq=seq),
        grid=(batch, A_HEADS),
        in_specs=[spec, spec, spec],
        out_specs=spec,
        out_shape=jax.ShapeDtypeStruct(q.shape, BF16),
        compiler_params=pltpu.CompilerParams(dimension_semantics=("arbitrary", "arbitrary"),
                                             vmem_limit_bytes=VMEM_LIMIT),
        name="moba_attn",
    )(q, k, v)


def _out_proj_kernel(a_ref, w_ref, x_ref, o_ref):
    o_ref[...] = x_ref[...] + _dot(a_ref[...], w_ref[...])


def _out_proj(a, w, x2):
    t, kdim = a.shape
    row = lambda i: (i, 0)
    return pl.pallas_call(
        _out_proj_kernel,
        grid=(t // ROW_TILE,),
        in_specs=[pl.BlockSpec((ROW_TILE, kdim), row), _resident((kdim, D_MODEL)),
                  pl.BlockSpec((ROW_TILE, D_MODEL), row)],
        out_specs=pl.BlockSpec((ROW_TILE, D_MODEL), row),
        out_shape=jax.ShapeDtypeStruct(x2.shape, F32),
        compiler_params=_params(),
        name="out_proj",
    )(a, w, x2)


def _mlp_kernel(x_ref, g_ref, w1_ref, w2_ref, o_ref, u_ref):
    x = x_ref[...]
    h = _rms(x, g_ref[...]).astype(BF16)
    for c in range(D_FF // D_MODEL):
        cols = slice(c * D_MODEL, (c + 1) * D_MODEL)
        u = jnp.maximum(_dot(h, w1_ref[:, cols]), 0.0)
        u_ref[:, cols] = (u * u).astype(BF16)
    o_ref[...] = x + _dot(u_ref[...], w2_ref[...])


def _mlp(x2, g, w1, w2):
    t = x2.shape[0]
    row = lambda i: (i, 0)
    return pl.pallas_call(
        _mlp_kernel,
        grid=(t // ROW_TILE,),
        in_specs=[pl.BlockSpec((ROW_TILE, D_MODEL), row), _resident((1, D_MODEL)),
                  _resident((D_MODEL, D_FF)), _resident((D_FF, D_MODEL))],
        out_specs=pl.BlockSpec((ROW_TILE, D_MODEL), row),
        out_shape=jax.ShapeDtypeStruct(x2.shape, F32),
        scratch_shapes=[pltpu.VMEM((ROW_TILE, D_FF), BF16)],
        compiler_params=_params(),
        name="mlp",
    )(x2, g, w1, w2)


def _ret_proj_kernel(x_ref, g_ref, w_ref, cos_ref, sin_ref, q_ref, k_ref, v_ref, gate_ref):
    h = _rms(x_ref[...], g_ref[...]).astype(BF16)
    cos = cos_ref[...]
    sin = sin_ref[...]
    half = R_QK_DIM // 2

    def rope_store(y, out_ref, scale):
        for hh in range(R_HEADS):
            lo = hh * R_QK_DIM
            x1 = y[:, lo:lo + half]
            x2 = y[:, lo + half:lo + R_QK_DIM]
            r1 = x1 * cos - x2 * sin
            r2 = x2 * cos + x1 * sin
            if scale is not None:
                r1, r2 = r1 * scale, r2 * scale
            out_ref[:, lo:lo + half] = r1.astype(BF16)
            out_ref[:, lo + half:lo + R_QK_DIM] = r2.astype(BF16)

    rope_store(_dot(h, w_ref[:, 0:D_MODEL]), q_ref, None)
    rope_store(_dot(h, w_ref[:, D_MODEL:2 * D_MODEL]), k_ref, R_QK_DIM ** -0.5)
    for c in range(2):
        cols = slice(c * D_MODEL, (c + 1) * D_MODEL)
        v_ref[:, cols] = _dot(h, w_ref[:, 2 * D_MODEL + c * D_MODEL:3 * D_MODEL + c * D_MODEL]).astype(BF16)
        gate_ref[:, cols] = _dot(h, w_ref[:, 4 * D_MODEL + c * D_MODEL:5 * D_MODEL + c * D_MODEL])


def _ret_proj(x2, g, w, cos, sin, seq):
    t = x2.shape[0]
    tiles_per_seq = seq // ROW_TILE
    row = lambda i: (i, 0)
    pos = lambda i: (i % tiles_per_seq, 0)
    half = R_QK_DIM // 2
    return pl.pallas_call(
        _ret_proj_kernel,
        grid=(t // ROW_TILE,),
        in_specs=[
            pl.BlockSpec((ROW_TILE, D_MODEL), row),
            _resident((1, D_MODEL)),
            _resident((D_MODEL, 6 * D_MODEL)),
            pl.BlockSpec((ROW_TILE, half), pos),
            pl.BlockSpec((ROW_TILE, half), pos),
        ],
        out_specs=[pl.BlockSpec((ROW_TILE, D_MODEL), row), pl.BlockSpec((ROW_TILE, D_MODEL), row),
                   pl.BlockSpec((ROW_TILE, 2 * D_MODEL), row), pl.BlockSpec((ROW_TILE, 2 * D_MODEL), row)],
        out_shape=[jax.ShapeDtypeStruct((t, D_MODEL), BF16), jax.ShapeDtypeStruct((t, D_MODEL), BF16),
                   jax.ShapeDtypeStruct((t, 2 * D_MODEL), BF16), jax.ShapeDtypeStruct((t, 2 * D_MODEL), F32)],
        compiler_params=_params(),
        name="ret_proj",
    )(x2, g, w, cos, sin)


def _retention_kernel(q_ref, k_ref, v_ref, gate_ref, dmat_ref, xi_ref, zeta_ref, gch_ref, o_ref, *, seq):
    c = R_CHUNK
    dmat = dmat_ref[0]
    xi = xi_ref[0]
    zeta = zeta_ref[0]
    gch = gch_ref[0]
    state = jnp.zeros((R_QK_DIM, R_V_DIM), F32)
    for n in range(seq // c):
        rows = slice(n * c, (n + 1) * c)
        q = q_ref[rows, :]
        k = k_ref[rows, :]
        v = v_ref[rows, :]
        s = _dot_nt(q, k) * dmat
        y = _dot(s.astype(BF16), v) + _dot(q, state.astype(BF16)) * xi
        kz = (k.astype(F32) * zeta).astype(BF16)
        state = state * gch + _dot_tn(kz, v)
        y = y * lax.rsqrt(jnp.mean(y * y, axis=-1, keepdims=True) + EPS)
        g = gate_ref[rows, :]
        o_ref[rows, :] = (g * jax.nn.sigmoid(g) * y).astype(BF16)


def _retention(q, k, v, gate, dmat, xi, zeta, gch, batch, seq):
    c = R_CHUNK
    qk_spec = pl.BlockSpec((seq, R_QK_DIM), lambda b, h: (b, h))
    v_spec = pl.BlockSpec((seq, R_V_DIM), lambda b, h: (b, h))
    return pl.pallas_call(
        functools.partial(_retention_kernel, seq=seq),
        grid=(batch, R_HEADS),
        in_specs=[qk_spec, qk_spec, v_spec, v_spec,
                  pl.BlockSpec((1, c, c), lambda b, h: (h, 0, 0)),
                  pl.BlockSpec((1, c, 1), lambda b, h: (h, 0, 0)),
                  pl.BlockSpec((1, c, 1), lambda b, h: (h, 0, 0)),
                  pl.BlockSpec((1, 1, 1), lambda b, h: (h, 0, 0))],
        out_specs=v_spec,
        out_shape=jax.ShapeDtypeStruct(v.shape, BF16),
        compiler_params=pltpu.CompilerParams(dimension_semantics=("arbitrary", "arbitrary"),
                                             vmem_limit_bytes=VMEM_LIMIT),
        name="retention",
    )(q, k, v, gate, dmat, xi, zeta, gch)


def _rope_tables(seq, half):
    inv = ROPE_THETA ** (-jnp.arange(half, dtype=F32) / half)
    ang = jnp.arange(seq, dtype=jnp.int32).astype(F32)[:, None] * inv[None, :]
    return jnp.cos(ang), jnp.sin(ang)


def _decay_tables():
    c = R_CHUNK
    log_g = jnp.log1p(-(2.0 ** (-5.0 - jnp.arange(R_HEADS, dtype=F32))))
    idx = jnp.arange(c, dtype=F32)
    diff = idx[:, None] - idx[None, :]
    dmat = jnp.where(diff >= 0, jnp.exp(log_g[:, None, None] * jnp.maximum(diff, 0.0)), 0.0)
    xi = jnp.exp(log_g[:, None] * (idx + 1.0))[:, :, None]
    zeta = jnp.exp(log_g[:, None] * (c - 1.0 - idx))[:, :, None]
    gch = jnp.exp(log_g * c)[:, None, None]
    return dmat, xi, zeta, gch


def kernel(x, norm_mix_g, norm_mlp_g, a_w_qkv, a_q_gain, a_k_gain, a_w_o, r_w_in, r_w_out, mlp_w1, mlp_w2):
    batch, seq, d = x.shape
    assert d == D_MODEL and seq % ROW_TILE == 0 and seq % MOBA_BLOCK == 0 and seq % R_CHUNK == 0
    depth = norm_mix_g.shape[0]
    x2 = x.reshape(batch * seq, d)

    cos_a, sin_a = _rope_tables(seq, A_HEAD_DIM // 2)
    cos_a = jnp.concatenate([cos_a, cos_a], axis=1)
    sin_a = jnp.concatenate([-sin_a, sin_a], axis=1)
    cos_r, sin_r = _rope_tables(seq, R_QK_DIM // 2)
    dmat, xi, zeta, gch = _decay_tables()

    for i in range(depth):
        j = i // 2
        g_mix = norm_mix_g[i][None, :]
        if i % 2 == 0:
            q, k, v = _moba_qkv(x2, g_mix, a_w_qkv[j].astype(BF16), a_q_gain[j][None, :], a_k_gain[j][None, :],
                                cos_a, sin_a, seq)
            a = _moba_attn(q, k, v, batch, seq)
            x2 = _out_proj(a, a_w_o[j].astype(BF16), x2)
        else:
            q, k, v, gate = _ret_proj(x2, g_mix, r_w_in[j].astype(BF16), cos_r, sin_r, seq)
            a = _retention(q, k, v, gate, dmat, xi, zeta, gch, batch, seq)
            x2 = _out_proj(a, r_w_out[j].astype(BF16), x2)
        x2 = _mlp(x2, norm_mlp_g[i][None, :], mlp_w1[i].astype(BF16), mlp_w2[i].astype(BF16))
    return x2.reshape(batch, seq, d)
```

```python
import functools

import jax
import jax.numpy as jnp
import numpy as np
from jax import lax
from jax.experimental import pallas as pl
from jax.experimental.pallas import tpu as pltpu

D_MODEL = 1024
A_HEADS = 8
A_HEAD_DIM = D_MODEL // A_HEADS
MOBA_BLOCK = 256
MOBA_TOPK = 3
R_HEADS = 4
R_QK_DIM = D_MODEL // R_HEADS
R_V_DIM = 2 * D_MODEL // R_HEADS
R_CHUNK = 128
D_FF = 4 * D_MODEL
ROPE_THETA = 10000.0
EPS = 1e-6
NEG = -1e30

LANES = 128
ROW_TILE = 512
VMEM_LIMIT = 56 * 1024 * 1024

BF16 = jnp.bfloat16
F32 = jnp.float32


def _dot(a, b):
    return jnp.dot(a, b, preferred_element_type=F32)


def _dot_nt(a, b):
    return lax.dot_general(a, b, (((1,), (1,)), ((), ())), preferred_element_type=F32)


def _dot_tn(a, b):
    return lax.dot_general(a, b, (((0,), (0,)), ((), ())), preferred_element_type=F32)


def _rms(x, g):
    return x * lax.rsqrt(jnp.mean(x * x, axis=-1, keepdims=True) + EPS) * g


def _params():
    return pltpu.CompilerParams(dimension_semantics=("arbitrary",), vmem_limit_bytes=VMEM_LIMIT)


def _resident(shape):
    nd = len(shape)
    return pl.BlockSpec(shape, lambda *_: (0,) * nd, pipeline_mode=pl.Buffered(1))


def _moba_qkv_kernel(x_ref, g_ref, w_ref, qg_ref, kg_ref, cos_ref, sin_ref, q_ref, k_ref, v_ref):
    h = _rms(x_ref[...], g_ref[...]).astype(BF16)
    cos = cos_ref[...]
    sin = sin_ref[...]
    half = A_HEAD_DIM // 2

    def rope_tables(gain_ref, scale):
        gain = jnp.broadcast_to(gain_ref[...], (8, A_HEAD_DIM))
        return cos * (gain[0:1] * scale), sin * (pltpu.roll(gain, half, axis=1)[0:1] * scale)

    tables = (rope_tables(qg_ref, A_HEAD_DIM ** -0.5 * float(np.log2(np.e))), rope_tables(kg_ref, 1.0))
    outs = (q_ref, k_ref, v_ref)
    chunk = 2 * A_HEAD_DIM
    n_chunks = 3 * D_MODEL // chunk

    def project(c):
        return _dot(h, w_ref[:, c * chunk:(c + 1) * chunk])

    def epilogue(c, y):
        section, lo = divmod(c * chunk, D_MODEL)
        if section == 2:
            v_ref[:, lo:lo + chunk] = y.astype(BF16)
            return
        a, b = tables[section]
        for j in range(chunk // A_HEAD_DIM):
            yh = y[:, j * A_HEAD_DIM:(j + 1) * A_HEAD_DIM]
            r = lax.rsqrt(jnp.mean(yh * yh, axis=-1, keepdims=True) + EPS)
            out = (yh * a + pltpu.roll(yh, half, axis=1) * b) * r
            outs[section][:, lo + j * A_HEAD_DIM:lo + (j + 1) * A_HEAD_DIM] = out.astype(BF16)

    pending = project(0)
    for c in range(n_chunks):
        following = project(c + 1) if c + 1 < n_chunks else None
        epilogue(c, pending)
        pending = following


def _moba_qkv(x2, g, w, qg, kg, cos, sin, seq):
    t = x2.shape[0]
    tiles_per_seq = seq // ROW_TILE
    row = lambda i: (i, 0)
    pos = lambda i: (i % tiles_per_seq, 0)
    out = jax.ShapeDtypeStruct((t, D_MODEL), BF16)
    return pl.pallas_call(
        _moba_qkv_kernel,
        grid=(t // ROW_TILE,),
        in_specs=[
            pl.BlockSpec((ROW_TILE, D_MODEL), row),
            _resident((1, D_MODEL)),
            _resident((D_MODEL, 3 * D_MODEL)),
            _resident((1, A_HEAD_DIM)),
            _resident((1, A_HEAD_DIM)),
            pl.BlockSpec((ROW_TILE, A_HEAD_DIM), pos),
            pl.BlockSpec((ROW_TILE, A_HEAD_DIM), pos),
        ],
        out_specs=[pl.BlockSpec((ROW_TILE, D_MODEL), row)] * 3,
        out_shape=[out, out, out],
        compiler_params=_params(),
        name="moba_qkv",
    )(x2, g, w, qg, kg, cos, sin)


def _moba_bias(q, k, seq):
    nb = seq // MOBA_BLOCK
    kmean = jnp.mean(k.astype(F32).reshape(nb, MOBA_BLOCK, A_HEAD_DIM), axis=1)
    k_hi = kmean.astype(BF16)
    k_lo = (kmean - k_hi.astype(F32)).astype(BF16)
    g2 = _dot_nt(jnp.concatenate([k_hi, k_lo], axis=0), q)
    gate = g2[:nb] + g2[nb:]
    blk = lax.broadcasted_iota(jnp.int32, (nb, seq), 0)
    own = lax.broadcasted_iota(jnp.int32, (nb, seq), 1) // MOBA_BLOCK
    gate = jnp.where(blk < own, gate, NEG)
    rank = jnp.zeros((nb, seq), jnp.int32)
    for j in range(nb):
        gj = gate[j:j + 1, :]
        beats = (gj > gate) | ((gj == gate) & (blk > j))
        rank = rank + beats.astype(jnp.int32)
    chosen = ((rank < MOBA_TOPK) & (blk < own)) | (blk == own)
    bias_t = jnp.where(chosen, 0.0, NEG).astype(F32)
    bias_t = jnp.concatenate([bias_t, jnp.zeros((LANES - nb, seq), F32)], axis=0)
    return bias_t.T.astype(BF16)


def _moba_attn_kernel(q_ref, k_ref, v_ref, o_ref, *, seq, heads):
    nb = seq // MOBA_BLOCK
    kb = lax.broadcasted_iota(jnp.int32, (seq, LANES), 0) // MOBA_BLOCK
    kc = lax.broadcasted_iota(jnp.int32, (seq, LANES), 1)
    onehot = (kb == kc).astype(BF16)
    ones = jnp.ones((seq, LANES), BF16)
    qi_idx = lax.broadcasted_iota(jnp.int32, (MOBA_BLOCK, MOBA_BLOCK), 0)
    ki_idx = lax.broadcasted_iota(jnp.int32, (MOBA_BLOCK, MOBA_BLOCK), 1)
    causal = ki_idx <= qi_idx

    q, k_aug, v_aug, q_aug = [], [], [], []
    for h in range(heads):
        cols = slice(h * A_HEAD_DIM, (h + 1) * A_HEAD_DIM)
        q.append(q_ref[:, cols])
        k_aug.append(jnp.concatenate([k_ref[:, cols], onehot], axis=1))
        v_aug.append(jnp.concatenate([v_ref[:, cols], ones], axis=1))

    def scores(i, h):
        rows = slice(i * MOBA_BLOCK, (i + 1) * MOBA_BLOCK)
        if i == 0:
            qa = jnp.concatenate([q[h][rows, :], jnp.zeros((MOBA_BLOCK, LANES), BF16)], axis=1)
        else:
            qa = q_aug[h][rows, :]
        return _dot_nt(qa, k_aug[h][:(i + 1) * MOBA_BLOCK, :])

    def finish(i, h, s):
        s_diag = jnp.where(causal, s[:, i * MOBA_BLOCK:], NEG)
        s = s_diag if i == 0 else jnp.concatenate([s[:, :i * MOBA_BLOCK], s_diag], axis=1)
        m = jnp.max(s, axis=-1, keepdims=True)
        p = jnp.exp2(s - m).astype(BF16)
        o = _dot(p, v_aug[h][:(i + 1) * MOBA_BLOCK, :])
        o_ref[i * MOBA_BLOCK:(i + 1) * MOBA_BLOCK, h * A_HEAD_DIM:(h + 1) * A_HEAD_DIM] = (
            o[:, :A_HEAD_DIM] / o[:, A_HEAD_DIM:A_HEAD_DIM + 1]).astype(BF16)

    tasks = [(i, h) for i in range(nb) for h in range(heads)]
    pending = [scores(0, h) for h in range(heads)]
    for h in range(heads):
        q_aug.append(jnp.concatenate([q[h], _moba_bias(q[h], k_aug[h][:, :A_HEAD_DIM], seq)], axis=1))
    for t, (i, h) in enumerate(tasks):
        if t + heads < len(tasks):
            pending.append(scores(*tasks[t + heads]))
        finish(i, h, pending[t])


MOBA_HEADS_PER_STEP = 4


def _moba_attn(q, k, v, batch, seq):
    hps = MOBA_HEADS_PER_STEP
    spec = pl.BlockSpec((seq, hps * A_HEAD_DIM), lambda b, h: (b, h))
    return pl.pallas_call(
        functools.partial(_moba_attn_kernel, seq=seq, heads=hps),
        grid=(batch, A_HEADS // hps),
        in_specs=[spec, spec, spec],
        out_specs=spec,
        out_shape=jax.ShapeDtypeStruct(q.shape, BF16),
        compiler_params=pltpu.CompilerParams(dimension_semantics=("arbitrary", "arbitrary"),
                                             vmem_limit_bytes=VMEM_LIMIT),
        name="moba_attn",
    )(q, k, v)


def _out_proj_kernel(a_ref, w_ref, x_ref, o_ref):
    o_ref[...] = x_ref[...] + _dot(a_ref[...], w_ref[...])


def _out_proj(a, w, x2):
    t, kdim = a.shape
    row = lambda i: (i, 0)
    return pl.pallas_call(
        _out_proj_kernel,
        grid=(t // ROW_TILE,),
        in_specs=[pl.BlockSpec((ROW_TILE, kdim), row), _resident((kdim, D_MODEL)),
                  pl.BlockSpec((ROW_TILE, D_MODEL), row)],
        out_specs=pl.BlockSpec((ROW_TILE, D_MODEL), row),
        out_shape=jax.ShapeDtypeStruct(x2.shape, F32),
        compiler_params=_params(),
        name="out_proj",
    )(a, w, x2)


def _mlp_kernel(x_ref, g_ref, w1_ref, w2_ref, o_ref, u_ref):
    x = x_ref[...]
    h = _rms(x, g_ref[...]).astype(BF16)
    for c in range(D_FF // D_MODEL):
        cols = slice(c * D_MODEL, (c + 1) * D_MODEL)
        u = jnp.maximum(_dot(h, w1_ref[:, cols]), 0.0)
        u_ref[:, cols] = (u * u).astype(BF16)
    o_ref[...] = x + _dot(u_ref[...], w2_ref[...])


def _mlp(x2, g, w1, w2):
    t = x2.shape[0]
    row = lambda i: (i, 0)
    return pl.pallas_call(
        _mlp_kernel,
        grid=(t // ROW_TILE,),
        in_specs=[pl.BlockSpec((ROW_TILE, D_MODEL), row), _resident((1, D_MODEL)),
                  _resident((D_MODEL, D_FF)), _resident((D_FF, D_MODEL))],
        out_specs=pl.BlockSpec((ROW_TILE, D_MODEL), row),
        out_shape=jax.ShapeDtypeStruct(x2.shape, F32),
        scratch_shapes=[pltpu.VMEM((ROW_TILE, D_FF), BF16)],
        compiler_params=_params(),
        name="mlp",
    )(x2, g, w1, w2)


def _ret_proj_kernel(x_ref, g_ref, w_ref, cos_ref, sin_ref, q_ref, k_ref, v_ref, gate_ref):
    h = _rms(x_ref[...], g_ref[...]).astype(BF16)
    cos = cos_ref[...]
    sin = sin_ref[...]
    half = R_QK_DIM // 2

    def rope_store(y, out_ref, scale):
        for hh in range(R_HEADS):
            lo = hh * R_QK_DIM
            x1 = y[:, lo:lo + half]
            x2 = y[:, lo + half:lo + R_QK_DIM]
            r1 = x1 * cos - x2 * sin
            r2 = x2 * cos + x1 * sin
            if scale is not None:
                r1, r2 = r1 * scale, r2 * scale
            out_ref[:, lo:lo + half] = r1.astype(BF16)
            out_ref[:, lo + half:lo + R_QK_DIM] = r2.astype(BF16)

    rope_store(_dot(h, w_ref[:, 0:D_MODEL]), q_ref, None)
    rope_store(_dot(h, w_ref[:, D_MODEL:2 * D_MODEL]), k_ref, R_QK_DIM ** -0.5)
    for c in range(2):
        cols = slice(c * D_MODEL, (c + 1) * D_MODEL)
        v_ref[:, cols] = _dot(h, w_ref[:, 2 * D_MODEL + c * D_MODEL:3 * D_MODEL + c * D_MODEL]).astype(BF16)
        gate_ref[:, cols] = _dot(h, w_ref[:, 4 * D_MODEL + c * D_MODEL:5 * D_MODEL + c * D_MODEL])


def _ret_proj(x2, g, w, cos, sin, seq):
    t = x2.shape[0]
    tiles_per_seq = seq // ROW_TILE
    row = lambda i: (i, 0)
    pos = lambda i: (i % tiles_per_seq, 0)
    half = R_QK_DIM // 2
    return pl.pallas_call(
        _ret_proj_kernel,
        grid=(t // ROW_TILE,),
        in_specs=[
            pl.BlockSpec((ROW_TILE, D_MODEL), row),
            _resident((1, D_MODEL)),
            _resident((D_MODEL, 6 * D_MODEL)),
            pl.BlockSpec((ROW_TILE, half), pos),
            pl.BlockSpec((ROW_TILE, half), pos),
        ],
        out_specs=[pl.BlockSpec((ROW_TILE, D_MODEL), row), pl.BlockSpec((ROW_TILE, D_MODEL), row),
                   pl.BlockSpec((ROW_TILE, 2 * D_MODEL), row), pl.BlockSpec((ROW_TILE, 2 * D_MODEL), row)],
        out_shape=[jax.ShapeDtypeStruct((t, D_MODEL), BF16), jax.ShapeDtypeStruct((t, D_MODEL), BF16),
                   jax.ShapeDtypeStruct((t, 2 * D_MODEL), BF16), jax.ShapeDtypeStruct((t, 2 * D_MODEL), F32)],
        compiler_params=_params(),
        name="ret_proj",
    )(x2, g, w, cos, sin)


def _retention_kernel(q_ref, k_ref, v_ref, gate_ref, dmat_ref, xi_ref, zeta_ref, gch_ref, o_ref, *, seq):
    c = R_CHUNK
    dmat = dmat_ref[0]
    xi = xi_ref[0]
    zeta = zeta_ref[0]
    gch = gch_ref[0]
    state = jnp.zeros((R_QK_DIM, R_V_DIM), F32)
    for n in range(seq // c):
        rows = slice(n * c, (n + 1) * c)
        q = q_ref[rows, :]
        k = k_ref[rows, :]
        v = v_ref[rows, :]
        s = _dot_nt(q, k) * dmat
        y = _dot(s.astype(BF16), v) + _dot(q, state.astype(BF16)) * xi
        kz = (k.astype(F32) * zeta).astype(BF16)
        state = state * gch + _dot_tn(kz, v)
        y = y * lax.rsqrt(jnp.mean(y * y, axis=-1, keepdims=True) + EPS)
        g = gate_ref[rows, :]
        o_ref[rows, :] = (g * jax.nn.sigmoid(g) * y).astype(BF16)


def _retention(q, k, v, gate, dmat, xi, zeta, gch, batch, seq):
    c = R_CHUNK
    qk_spec = pl.BlockSpec((seq, R_QK_DIM), lambda b, h: (b, h))
    v_spec = pl.BlockSpec((seq, R_V_DIM), lambda b, h: (b, h))
    return pl.pallas_call(
        functools.partial(_retention_kernel, seq=seq),
        grid=(batch, R_HEADS),
        in_specs=[qk_spec, qk_spec, v_spec, v_spec,
                  pl.BlockSpec((1, c, c), lambda b, h: (h, 0, 0)),
                  pl.BlockSpec((1, c, 1), lambda b, h: (h, 0, 0)),
                  pl.BlockSpec((1, c, 1), lambda b, h: (h, 0, 0)),
                  pl.BlockSpec((1, 1, 1), lambda b, h: (h, 0, 0))],
        out_specs=v_spec,
        out_shape=jax.ShapeDtypeStruct(v.shape, BF16),
        compiler_params=pltpu.CompilerParams(dimension_semantics=("arbitrary", "arbitrary"),
                                             vmem_limit_bytes=VMEM_LIMIT),
        name="retention",
    )(q, k, v, gate, dmat, xi, zeta, gch)


def _rope_tables(seq, half):
    inv = ROPE_THETA ** (-jnp.arange(half, dtype=F32) / half)
    ang = jnp.arange(seq, dtype=jnp.int32).astype(F32)[:, None] * inv[None, :]
    return jnp.cos(ang), jnp.sin(ang)


def _decay_tables():
    c = R_CHUNK
    log_g = jnp.log1p(-(2.0 ** (-5.0 - jnp.arange(R_HEADS, dtype=F32))))
    idx = jnp.arange(c, dtype=F32)
    diff = idx[:, None] - idx[None, :]
    dmat = jnp.where(diff >= 0, jnp.exp(log_g[:, None, None] * jnp.maximum(diff, 0.0)), 0.0)
    xi = jnp.exp(log_g[:, None] * (idx + 1.0))[:, :, None]
    zeta = jnp.exp(log_g[:, None] * (c - 1.0 - idx))[:, :, None]
    gch = jnp.exp(log_g * c)[:, None, None]
    return dmat, xi, zeta, gch


def kernel(x, norm_mix_g, norm_mlp_g, a_w_qkv, a_q_gain, a_k_gain, a_w_o, r_w_in, r_w_out, mlp_w1, mlp_w2):
    batch, seq, d = x.shape
    assert d == D_MODEL and seq % ROW_TILE == 0 and seq % MOBA_BLOCK == 0 and seq % R_CHUNK == 0
    depth = norm_mix_g.shape[0]
    x2 = x.reshape(batch * seq, d)

    cos_a, sin_a = _rope_tables(seq, A_HEAD_DIM // 2)
    cos_a = jnp.concatenate([cos_a, cos_a], axis=1)
    sin_a = jnp.concatenate([-sin_a, sin_a], axis=1)
    cos_r, sin_r = _rope_tables(seq, R_QK_DIM // 2)
    dmat, xi, zeta, gch = _decay_tables()

    for i in range(depth):
        j = i // 2
        g_mix = norm_mix_g[i][None, :]
        if i % 2 == 0:
            q, k, v = _moba_qkv(x2, g_mix, a_w_qkv[j].astype(BF16), a_q_gain[j][None, :], a_k_gain[j][None, :],
                                cos_a, sin_a, seq)
            a = _moba_attn(q, k, v, batch, seq)
            x2 = _out_proj(a, a_w_o[j].astype(BF16), x2)
        else:
            q, k, v, gate = _ret_proj(x2, g_mix, r_w_in[j].astype(BF16), cos_r, sin_r, seq)
            a = _retention(q, k, v, gate, dmat, xi, zeta, gch, batch, seq)
            x2 = _out_proj(a, r_w_out[j].astype(BF16), x2)
        x2 = _mlp(x2, norm_mlp_g[i][None, :], mlp_w1[i].astype(BF16), mlp_w2[i].astype(BF16))
    return x2.reshape(batch, seq, d)
```

```python
import functools

import jax
import jax.numpy as jnp
import numpy as np
from jax import lax
from jax.experimental import pallas as pl
from jax.experimental.pallas import tpu as pltpu

D_MODEL = 1024
A_HEADS = 8
A_HEAD_DIM = D_MODEL // A_HEADS
MOBA_BLOCK = 256
MOBA_TOPK = 3
R_HEADS = 4
R_QK_DIM = D_MODEL // R_HEADS
R_V_DIM = 2 * D_MODEL // R_HEADS
D_FF = 4 * D_MODEL
ROPE_THETA = 10000.0
EPS = 1e-6
NEG = -1e30

LANES = 128
ROW_TILE = 512
VMEM_LIMIT = 56 * 1024 * 1024

BF16 = jnp.bfloat16
F32 = jnp.float32


def _dot(a, b):
    return jnp.dot(a, b, preferred_element_type=F32)


def _dot_nt(a, b):
    return lax.dot_general(a, b, (((1,), (1,)), ((), ())), preferred_element_type=F32)


def _dot_tn(a, b):
    return lax.dot_general(a, b, (((0,), (0,)), ((), ())), preferred_element_type=F32)


def _rms(x, g):
    return x * lax.rsqrt(jnp.mean(x * x, axis=-1, keepdims=True) + EPS) * g


def _params():
    return pltpu.CompilerParams(dimension_semantics=("arbitrary",), vmem_limit_bytes=VMEM_LIMIT)


def _resident(shape):
    nd = len(shape)
    return pl.BlockSpec(shape, lambda *_: (0,) * nd, pipeline_mode=pl.Buffered(1))


def _pair_layout(w):
    lead = w.shape[:-1]
    half = A_HEAD_DIM // 2
    w = w.reshape(*lead, A_HEADS // 2, 2, 2, half)
    return jnp.swapaxes(w, -2, -3).reshape(*lead, D_MODEL)


def _moba_qkv_kernel(x_ref, g_ref, w_ref, qg_ref, kg_ref, cos_ref, sin_ref, q_ref, k_ref, v_ref):
    h = _rms(x_ref[...], g_ref[...]).astype(BF16)
    cos = cos_ref[...]
    sin = sin_ref[...]
    first = lax.broadcasted_iota(jnp.int32, (1, LANES), 1) < A_HEAD_DIM // 2

    def rope_tables(gain_ref, scale):
        g1 = gain_ref[0:1, :] * scale
        g2 = gain_ref[1:2, :] * scale
        return cos * g1, sin * g2, cos * g2, sin * g1

    tables = (rope_tables(qg_ref, A_HEAD_DIM ** -0.5 * float(np.log2(np.e))), rope_tables(kg_ref, 1.0))
    outs = (q_ref, k_ref, v_ref)
    chunk = 2 * A_HEAD_DIM
    n_chunks = 3 * D_MODEL // chunk

    def project(c):
        return _dot(h, w_ref[:, c * chunk:(c + 1) * chunk])

    def head_rsqrt(t, lanes):
        return lax.rsqrt(jnp.sum(jnp.where(lanes, t, 0.0), axis=-1, keepdims=True) * (1.0 / A_HEAD_DIM) + EPS)

    def epilogue(c, y):
        section, lo = divmod(c * chunk, D_MODEL)
        if section == 2:
            v_ref[:, lo:lo + chunk] = y.astype(BF16)
            return
        c1, s2, c2, s1 = tables[section]
        y1 = y[:, :LANES]
        y2 = y[:, LANES:]
        t = y1 * y1 + y2 * y2
        r = jnp.where(first, head_rsqrt(t, first), head_rsqrt(t, ~first))
        outs[section][:, lo:lo + LANES] = ((y1 * c1 - y2 * s2) * r).astype(BF16)
        outs[section][:, lo + LANES:lo + chunk] = ((y2 * c2 + y1 * s1) * r).astype(BF16)

    pending = project(0)
    for c in range(n_chunks):
        following = project(c + 1) if c + 1 < n_chunks else None
        epilogue(c, pending)
        pending = following


def _moba_qkv(x2, g, w, qg, kg, cos, sin, seq):
    t = x2.shape[0]
    tiles_per_seq = seq // ROW_TILE
    row = lambda i: (i, 0)
    pos = lambda i: (i % tiles_per_seq, 0)
    out = jax.ShapeDtypeStruct((t, D_MODEL), BF16)
    return pl.pallas_call(
        _moba_qkv_kernel,
        grid=(t // ROW_TILE,),
        in_specs=[
            pl.BlockSpec((ROW_TILE, D_MODEL), row),
            _resident((1, D_MODEL)),
            _resident((D_MODEL, 3 * D_MODEL)),
            _resident((2, A_HEAD_DIM)),
            _resident((2, A_HEAD_DIM)),
            pl.BlockSpec((ROW_TILE, A_HEAD_DIM), pos),
            pl.BlockSpec((ROW_TILE, A_HEAD_DIM), pos),
        ],
        out_specs=[pl.BlockSpec((ROW_TILE, D_MODEL), row)] * 3,
        out_shape=[out, out, out],
        compiler_params=_params(),
        name="moba_qkv",
    )(x2, g, w, qg, kg, cos, sin)


def _moba_bias(q, kmean, seq, lane_offset):
    nb = seq // MOBA_BLOCK
    k_hi = kmean.astype(BF16)
    k_lo = (kmean - k_hi.astype(F32)).astype(BF16)
    g2 = _dot_nt(jnp.concatenate([k_hi, k_lo], axis=0), q)
    gate = g2[:nb] + g2[nb:]
    blk = lax.broadcasted_iota(jnp.int32, (nb, seq), 0)
    own = lax.broadcasted_iota(jnp.int32, (nb, seq), 1) // MOBA_BLOCK
    gate = jnp.where(blk < own, gate, NEG)
    rank = jnp.zeros((nb, seq), jnp.int32)
    for j in range(nb):
        gj = gate[j:j + 1, :]
        beats = (gj > gate) | ((gj == gate) & (blk > j))
        rank = rank + beats.astype(jnp.int32)
    chosen = ((rank < MOBA_TOPK) & (blk < own)) | (blk == own)
    bias_t = jnp.where(chosen, 0.0, NEG).astype(F32)
    pad = [jnp.zeros((rows, seq), F32) for rows in (lane_offset, LANES - nb - lane_offset)]
    parts = [part for part in (pad[0], bias_t, pad[1]) if part.shape[0]]
    return jnp.concatenate(parts, axis=0).T.astype(BF16)


def _moba_attn_kernel(q_ref, k_ref, v_ref, o_ref, *, seq, heads):
    nb = seq // MOBA_BLOCK
    half = A_HEAD_DIM // 2
    lane = lax.broadcasted_iota(jnp.int32, (seq, LANES), 1)
    key_block = lax.broadcasted_iota(jnp.int32, (seq, LANES), 0) // MOBA_BLOCK
    first = lane < half
    ones = jnp.ones((seq, LANES), BF16)
    qi_idx = lax.broadcasted_iota(jnp.int32, (MOBA_BLOCK, MOBA_BLOCK), 0)
    ki_idx = lax.broadcasted_iota(jnp.int32, (MOBA_BLOCK, MOBA_BLOCK), 1)
    causal = ki_idx <= qi_idx

    q, k_aug, v_aug, q_aug, kmeans, offsets = [], [], [], [], [], []
    for h in range(heads):
        pair, second = divmod(h, 2)
        lo = pair * 2 * A_HEAD_DIM
        own = ~first if second else first
        offset = 0 if second else half
        bias_lanes = (lane >= offset) & (lane < offset + nb)
        onehot = (lane == key_block + offset).astype(BF16)
        zero = jnp.zeros((seq, LANES), BF16)
        q1 = jnp.where(own, q_ref[:, lo:lo + LANES], zero)
        q2 = jnp.where(own, q_ref[:, lo + LANES:lo + 2 * LANES], zero)
        q.append((q1, q2))
        k_aug.append(jnp.concatenate([jnp.where(bias_lanes, onehot, k_ref[:, lo:lo + LANES]),
                                      k_ref[:, lo + LANES:lo + 2 * LANES]], axis=1))
        v_aug.append(jnp.concatenate([v_ref[:, h * A_HEAD_DIM:(h + 1) * A_HEAD_DIM], ones], axis=1))
        offsets.append(offset)
        if not second:
            k_pair = k_ref[:, lo:lo + 2 * LANES].astype(F32)
            kmeans.append(jnp.mean(k_pair.reshape(nb, MOBA_BLOCK, 2 * LANES), axis=1))

    def scores(i, h):
        rows = slice(i * MOBA_BLOCK, (i + 1) * MOBA_BLOCK)
        if i == 0:
            qa = jnp.concatenate([q[h][0][rows, :], q[h][1][rows, :]], axis=1)
        else:
            qa = q_aug[h][rows, :]
        return _dot_nt(qa, k_aug[h][:(i + 1) * MOBA_BLOCK, :])

    def finish(i, h, s):
        s_diag = jnp.where(causal, s[:, i * MOBA_BLOCK:], NEG)
        s = s_diag if i == 0 else jnp.concatenate([s[:, :i * MOBA_BLOCK], s_diag], axis=1)
        m = jnp.max(s, axis=-1, keepdims=True)
        p = jnp.exp2(s - m).astype(BF16)
        o = _dot(p, v_aug[h][:(i + 1) * MOBA_BLOCK, :])
        o_ref[i * MOBA_BLOCK:(i + 1) * MOBA_BLOCK, h * A_HEAD_DIM:(h + 1) * A_HEAD_DIM] = (
            o[:, :A_HEAD_DIM] / o[:, A_HEAD_DIM:A_HEAD_DIM + 1]).astype(BF16)

    tasks = [(i, h) for i in range(nb) for h in range(heads)]
    pending = [scores(0, h) for h in range(heads)]
    for h in range(heads):
        bias = _moba_bias(jnp.concatenate(q[h], axis=1), kmeans[h // 2], seq, offsets[h])
        q_aug.append(jnp.concatenate([q[h][0] + bias, q[h][1]], axis=1))
    for t, (i, h) in enumerate(tasks):
        if t + heads < len(tasks):
            pending.append(scores(*tasks[t + heads]))
        finish(i, h, pending[t])


MOBA_HEADS_PER_STEP = 4


def _moba_attn(q, k, v, batch, seq):
    hps = MOBA_HEADS_PER_STEP
    spec = pl.BlockSpec((seq, hps * A_HEAD_DIM), lambda b, h: (b, h))
    return pl.pallas_call(
        functools.partial(_moba_attn_kernel, seq=seq, heads=hps),
        grid=(batch, A_HEADS // hps),
        in_specs=[spec, spec, spec],
        out_specs=spec,
        out_shape=jax.ShapeDtypeStruct(q.shape, BF16),
        compiler_params=pltpu.CompilerParams(dimension_semantics=("arbitrary", "arbitrary"),
                                             vmem_limit_bytes=VMEM_LIMIT),
        name="moba_attn",
    )(q, k, v)


def _mix_mlp_kernel(a_ref, wo_ref, x_ref, g_ref, w1_ref, w2_ref, o_ref, u_ref):
    x1 = x_ref[...] + _dot(a_ref[...], wo_ref[...])
    h = _rms(x1, g_ref[...]).astype(BF16)
    for c in range(D_FF // D_MODEL):
        cols = slice(c * D_MODEL, (c + 1) * D_MODEL)
        u = jnp.maximum(_dot(h, w1_ref[:, cols]), 0.0)
        u_ref[:, cols] = (u * u).astype(BF16)
    o_ref[...] = x1 + _dot(u_ref[...], w2_ref[...])


def _mix_mlp(a, wo, x2, g, w1, w2):
    t, kdim = a.shape
    row = lambda i: (i, 0)
    return pl.pallas_call(
        _mix_mlp_kernel,
        grid=(t // ROW_TILE,),
        in_specs=[pl.BlockSpec((ROW_TILE, kdim), row), _resident((kdim, D_MODEL)),
                  pl.BlockSpec((ROW_TILE, D_MODEL), row), _resident((1, D_MODEL)),
                  _resident((D_MODEL, D_FF)), _resident((D_FF, D_MODEL))],
        out_specs=pl.BlockSpec((ROW_TILE, D_MODEL), row),
        out_shape=jax.ShapeDtypeStruct(x2.shape, F32),
        scratch_shapes=[pltpu.VMEM((ROW_TILE, D_FF), BF16)],
        compiler_params=_params(),
        name="mix_mlp",
    )(a, wo, x2, g, w1, w2)


RET_PROJ_CHUNK = 512


def _ret_proj_kernel(x_ref, g_ref, w_ref, cq_ref, sq_ref, ck_ref, sk_ref, q_ref, k_ref, v_ref, sg_ref):
    h = _rms(x_ref[...], g_ref[...]).astype(BF16)
    half = R_QK_DIM // 2
    chunk = RET_PROJ_CHUNK
    n_chunks = 6 * D_MODEL // chunk

    def project(c):
        return _dot(h, w_ref[:, c * chunk:(c + 1) * chunk])

    def rope_store(y, lo, out_ref, cos_ref, sin_ref):
        for j in range(chunk // R_QK_DIM):
            hh = lo // R_QK_DIM + j
            cos = cos_ref[hh]
            sin = sin_ref[hh]
            x1 = y[:, j * R_QK_DIM:j * R_QK_DIM + half]
            x2 = y[:, j * R_QK_DIM + half:(j + 1) * R_QK_DIM]
            out_ref[:, hh * R_QK_DIM:hh * R_QK_DIM + half] = (x1 * cos - x2 * sin).astype(BF16)
            out_ref[:, hh * R_QK_DIM + half:(hh + 1) * R_QK_DIM] = (x2 * cos + x1 * sin).astype(BF16)

    def epilogue(c, y):
        col = c * chunk
        if col < D_MODEL:
            rope_store(y, col, q_ref, cq_ref, sq_ref)
        elif col < 2 * D_MODEL:
            rope_store(y, col - D_MODEL, k_ref, ck_ref, sk_ref)
        elif col < 4 * D_MODEL:
            v_ref[:, col - 2 * D_MODEL:col - 2 * D_MODEL + chunk] = y.astype(BF16)
        else:
            sg_ref[:, col - 4 * D_MODEL:col - 4 * D_MODEL + chunk] = (y * jax.nn.sigmoid(y)).astype(BF16)

    pending = project(0)
    for c in range(n_chunks):
        following = project(c + 1) if c + 1 < n_chunks else None
        epilogue(c, pending)
        pending = following


def _ret_proj(x2, g, w, tables, seq):
    t = x2.shape[0]
    tiles_per_seq = seq // ROW_TILE
    row = lambda i: (i, 0)
    half = R_QK_DIM // 2
    table = pl.BlockSpec((R_HEADS, ROW_TILE, half), lambda i: (0, i % tiles_per_seq, 0))
    return pl.pallas_call(
        _ret_proj_kernel,
        grid=(t // ROW_TILE,),
        in_specs=[pl.BlockSpec((ROW_TILE, D_MODEL), row), _resident((1, D_MODEL)),
                  _resident((D_MODEL, 6 * D_MODEL)), table, table, table, table],
        out_specs=[pl.BlockSpec((ROW_TILE, D_MODEL), row), pl.BlockSpec((ROW_TILE, D_MODEL), row),
                   pl.BlockSpec((ROW_TILE, 2 * D_MODEL), row), pl.BlockSpec((ROW_TILE, 2 * D_MODEL), row)],
        out_shape=[jax.ShapeDtypeStruct((t, D_MODEL), BF16), jax.ShapeDtypeStruct((t, D_MODEL), BF16),
                   jax.ShapeDtypeStruct((t, 2 * D_MODEL), BF16), jax.ShapeDtypeStruct((t, 2 * D_MODEL), BF16)],
        compiler_params=_params(),
        name="ret_proj",
    )(x2, g, w, *tables)


RET_CHUNK = 256


def _retention_kernel(q_ref, k_ref, v_ref, sg_ref, gch_ref, o_ref, *, seq):
    c = RET_CHUNK
    n_chunks = seq // c
    gch = gch_ref[0]
    ti = lax.broadcasted_iota(jnp.int32, (c, c), 0)
    ui = lax.broadcasted_iota(jnp.int32, (c, c), 1)
    causal = ui <= ti

    def chunk_matmuls(n):
        rows = slice(n * c, (n + 1) * c)
        return _dot_nt(q_ref[rows, :], k_ref[rows, :]), _dot_tn(k_ref[rows, :], v_ref[rows, :])

    decayed = None
    pending = chunk_matmuls(0)
    for n in range(n_chunks):
        rows = slice(n * c, (n + 1) * c)
        following = chunk_matmuls(n + 1) if n + 1 < n_chunks else None
        s, update = pending
        y = _dot(jnp.where(causal, s, 0.0).astype(BF16), v_ref[rows, :])
        if decayed is not None:
            y = y + _dot(q_ref[rows, :], decayed.astype(BF16))
        if n + 1 < n_chunks:
            decayed = (update if decayed is None else decayed + update) * gch
        y = y * lax.rsqrt(jnp.mean(y * y, axis=-1, keepdims=True) + EPS)
        o_ref[rows, :] = (sg_ref[rows, :].astype(F32) * y).astype(BF16)
        pending = following


def _retention(q, k, v, sg, gch, batch, seq):
    qk_spec = pl.BlockSpec((seq, R_QK_DIM), lambda b, h: (b, h))
    v_spec = pl.BlockSpec((seq, R_V_DIM), lambda b, h: (b, h))
    return pl.pallas_call(
        functools.partial(_retention_kernel, seq=seq),
        grid=(batch, R_HEADS),
        in_specs=[qk_spec, qk_spec, v_spec, v_spec, pl.BlockSpec((1, 1, 1), lambda b, h: (h, 0, 0))],
        out_specs=v_spec,
        out_shape=jax.ShapeDtypeStruct(v.shape, BF16),
        compiler_params=pltpu.CompilerParams(dimension_semantics=("arbitrary", "arbitrary"),
                                             vmem_limit_bytes=VMEM_LIMIT),
        name="retention",
    )(q, k, v, sg, gch)


def _rope_tables(seq, half):
    inv = ROPE_THETA ** (-jnp.arange(half, dtype=F32) / half)
    ang = jnp.arange(seq, dtype=jnp.int32).astype(F32)[:, None] * inv[None, :]
    return jnp.cos(ang), jnp.sin(ang)


def _retention_tables(seq):
    cos, sin = _rope_tables(seq, R_QK_DIM // 2)
    log_g = jnp.log1p(-(2.0 ** (-5.0 - jnp.arange(R_HEADS, dtype=F32))))
    local = (jnp.arange(seq, dtype=jnp.int32) % RET_CHUNK).astype(F32)
    dq = jnp.exp(log_g[:, None] * local[None, :])[:, :, None]
    dk = jnp.exp(-log_g[:, None] * local[None, :])[:, :, None] * (R_QK_DIM ** -0.5)
    gch = jnp.exp(log_g * RET_CHUNK)[:, None, None]
    return (cos[None] * dq, sin[None] * dq, cos[None] * dk, sin[None] * dk), gch


def kernel(x, norm_mix_g, norm_mlp_g, a_w_qkv, a_q_gain, a_k_gain, a_w_o, r_w_in, r_w_out, mlp_w1, mlp_w2):
    batch, seq, d = x.shape
    assert d == D_MODEL and seq % ROW_TILE == 0 and seq % MOBA_BLOCK == 0 and seq % RET_CHUNK == 0
    depth = norm_mix_g.shape[0]
    x2 = x.reshape(batch * seq, d)

    cos_a, sin_a = _rope_tables(seq, A_HEAD_DIM // 2)
    cos_a = jnp.concatenate([cos_a, cos_a], axis=1)
    sin_a = jnp.concatenate([sin_a, sin_a], axis=1)
    ret_tables, gch = _retention_tables(seq)

    def pair_gain(gain):
        halves = gain.reshape(2, A_HEAD_DIM // 2)
        return jnp.concatenate([halves, halves], axis=1)

    for i in range(depth):
        j = i // 2
        g_mix = norm_mix_g[i][None, :]
        if i % 2 == 0:
            w_q, w_k, w_v = jnp.split(a_w_qkv[j], 3, axis=1)
            w_qkv = jnp.concatenate([_pair_layout(w_q), _pair_layout(w_k), w_v], axis=1).astype(BF16)
            q, k, v = _moba_qkv(x2, g_mix, w_qkv, pair_gain(a_q_gain[j]), pair_gain(a_k_gain[j]),
                                cos_a, sin_a, seq)
            a = _moba_attn(q, k, v, batch, seq)
            w_o = a_w_o[j]
        else:
            q, k, v, sg = _ret_proj(x2, g_mix, r_w_in[j].astype(BF16), ret_tables, seq)
            a = _retention(q, k, v, sg, gch, batch, seq)
            w_o = r_w_out[j]
        x2 = _mix_mlp(a, w_o.astype(BF16), x2, norm_mlp_g[i][None, :], mlp_w1[i].astype(BF16), mlp_w2[i].astype(BF16))
    return x2.reshape(batch, seq, d)
```

```python
import functools

import jax
import jax.numpy as jnp
import numpy as np
from jax import lax
from jax.experimental import pallas as pl
from jax.experimental.pallas import tpu as pltpu

D_MODEL = 1024
A_HEADS = 8
A_HEAD_DIM = D_MODEL // A_HEADS
MOBA_BLOCK = 256
MOBA_TOPK = 3
R_HEADS = 4
R_QK_DIM = D_MODEL // R_HEADS
R_V_DIM = 2 * D_MODEL // R_HEADS
D_FF = 4 * D_MODEL
ROPE_THETA = 10000.0
EPS = 1e-6
NEG = -1e30

LANES = 128
ROW_TILE = 512
VMEM_LIMIT = 56 * 1024 * 1024

BF16 = jnp.bfloat16
F32 = jnp.float32


def _dot(a, b):
    return jnp.dot(a, b, preferred_element_type=F32)


def _dot_nt(a, b):
    return lax.dot_general(a, b, (((1,), (1,)), ((), ())), preferred_element_type=F32)


def _dot_tn(a, b):
    return lax.dot_general(a, b, (((0,), (0,)), ((), ())), preferred_element_type=F32)


def _rms(x, g):
    return x * lax.rsqrt(jnp.mean(x * x, axis=-1, keepdims=True) + EPS) * g


def _params():
    return pltpu.CompilerParams(dimension_semantics=("arbitrary",), vmem_limit_bytes=VMEM_LIMIT)


def _resident(shape):
    nd = len(shape)
    return pl.BlockSpec(shape, lambda *_: (0,) * nd, pipeline_mode=pl.Buffered(1))


def _pair_layout(w):
    lead = w.shape[:-1]
    half = A_HEAD_DIM // 2
    w = w.reshape(*lead, A_HEADS // 2, 2, 2, half)
    return jnp.swapaxes(w, -2, -3).reshape(*lead, D_MODEL)


def _moba_qkv_kernel(x_ref, g_ref, w_ref, qg_ref, kg_ref, cos_ref, sin_ref, q_ref, k_ref, v_ref):
    h = _rms(x_ref[...], g_ref[...]).astype(BF16)
    cos = cos_ref[...]
    sin = sin_ref[...]
    first = lax.broadcasted_iota(jnp.int32, (1, LANES), 1) < A_HEAD_DIM // 2

    def rope_tables(gain_ref, scale):
        g1 = gain_ref[0:1, :] * scale
        g2 = gain_ref[1:2, :] * scale
        return cos * g1, sin * g2, cos * g2, sin * g1

    tables = (rope_tables(qg_ref, A_HEAD_DIM ** -0.5 * float(np.log2(np.e))), rope_tables(kg_ref, 1.0))
    outs = (q_ref, k_ref, v_ref)
    chunk = 2 * A_HEAD_DIM
    n_chunks = 3 * D_MODEL // chunk

    def project(c):
        return _dot(h, w_ref[:, c * chunk:(c + 1) * chunk])

    def head_rsqrt(t, lanes):
        return lax.rsqrt(jnp.sum(jnp.where(lanes, t, 0.0), axis=-1, keepdims=True) * (1.0 / A_HEAD_DIM) + EPS)

    def epilogue(c, y):
        section, lo = divmod(c * chunk, D_MODEL)
        if section == 2:
            v_ref[:, lo:lo + chunk] = y.astype(BF16)
            return
        c1, s2, c2, s1 = tables[section]
        y1 = y[:, :LANES]
        y2 = y[:, LANES:]
        t = y1 * y1 + y2 * y2
        r = jnp.where(first, head_rsqrt(t, first), head_rsqrt(t, ~first))
        outs[section][:, lo:lo + LANES] = ((y1 * c1 - y2 * s2) * r).astype(BF16)
        outs[section][:, lo + LANES:lo + chunk] = ((y2 * c2 + y1 * s1) * r).astype(BF16)

    pending = project(0)
    for c in range(n_chunks):
        following = project(c + 1) if c + 1 < n_chunks else None
        epilogue(c, pending)
        pending = following


def _moba_qkv(x2, g, w, qg, kg, cos, sin, seq):
    t = x2.shape[0]
    tiles_per_seq = seq // ROW_TILE
    row = lambda i: (i, 0)
    pos = lambda i: (i % tiles_per_seq, 0)
    out = jax.ShapeDtypeStruct((t, D_MODEL), BF16)
    return pl.pallas_call(
        _moba_qkv_kernel,
        grid=(t // ROW_TILE,),
        in_specs=[
            pl.BlockSpec((ROW_TILE, D_MODEL), row),
            _resident((1, D_MODEL)),
            _resident((D_MODEL, 3 * D_MODEL)),
            _resident((2, A_HEAD_DIM)),
            _resident((2, A_HEAD_DIM)),
            pl.BlockSpec((ROW_TILE, A_HEAD_DIM), pos),
            pl.BlockSpec((ROW_TILE, A_HEAD_DIM), pos),
        ],
        out_specs=[pl.BlockSpec((ROW_TILE, D_MODEL), row)] * 3,
        out_shape=[out, out, out],
        compiler_params=_params(),
        name="moba_qkv",
    )(x2, g, w, qg, kg, cos, sin)


def _moba_bias(q, kmean, seq, lane_offset):
    nb = seq // MOBA_BLOCK
    k_hi = kmean.astype(BF16)
    k_lo = (kmean - k_hi.astype(F32)).astype(BF16)
    g2 = _dot_nt(jnp.concatenate([k_hi, k_lo], axis=0), q)
    gate = g2[:nb] + g2[nb:]
    blk = lax.broadcasted_iota(jnp.int32, (nb, seq), 0)
    own = lax.broadcasted_iota(jnp.int32, (nb, seq), 1) // MOBA_BLOCK
    gate = jnp.where(blk < own, gate, NEG)
    rank = jnp.zeros((nb, seq), jnp.int32)
    for j in range(nb):
        gj = gate[j:j + 1, :]
        beats = (gj > gate) | ((gj == gate) & (blk > j))
        rank = rank + beats.astype(jnp.int32)
    chosen = ((rank < MOBA_TOPK) & (blk < own)) | (blk == own)
    bias_t = jnp.where(chosen, 0.0, NEG).astype(F32)
    pad = [jnp.zeros((rows, seq), F32) for rows in (lane_offset, LANES - nb - lane_offset)]
    parts = [part for part in (pad[0], bias_t, pad[1]) if part.shape[0]]
    return jnp.concatenate(parts, axis=0).T.astype(BF16)


def _moba_attn_kernel(q_ref, k_ref, v_ref, o_ref, *, seq, heads):
    nb = seq // MOBA_BLOCK
    half = A_HEAD_DIM // 2
    lane = lax.broadcasted_iota(jnp.int32, (seq, LANES), 1)
    key_block = lax.broadcasted_iota(jnp.int32, (seq, LANES), 0) // MOBA_BLOCK
    first = lane < half
    ones = jnp.ones((seq, LANES), BF16)
    qi_idx = lax.broadcasted_iota(jnp.int32, (MOBA_BLOCK, MOBA_BLOCK), 0)
    ki_idx = lax.broadcasted_iota(jnp.int32, (MOBA_BLOCK, MOBA_BLOCK), 1)
    causal = ki_idx <= qi_idx

    q, k_aug, v_aug, q_aug, kmeans, offsets = [], [], [], [], [], []
    for h in range(heads):
        pair, second = divmod(h, 2)
        lo = pair * 2 * A_HEAD_DIM
        own = ~first if second else first
        offset = 0 if second else half
        bias_lanes = (lane >= offset) & (lane < offset + nb)
        onehot = (lane == key_block + offset).astype(BF16)
        zero = jnp.zeros((seq, LANES), BF16)
        q1 = jnp.where(own, q_ref[:, lo:lo + LANES], zero)
        q2 = jnp.where(own, q_ref[:, lo + LANES:lo + 2 * LANES], zero)
        q.append((q1, q2))
        k_aug.append(jnp.concatenate([jnp.where(bias_lanes, onehot, k_ref[:, lo:lo + LANES]),
                                      k_ref[:, lo + LANES:lo + 2 * LANES]], axis=1))
        v_aug.append(jnp.concatenate([v_ref[:, h * A_HEAD_DIM:(h + 1) * A_HEAD_DIM], ones], axis=1))
        offsets.append(offset)
        if not second:
            k_pair = k_ref[:, lo:lo + 2 * LANES].astype(F32)
            kmeans.append(jnp.mean(k_pair.reshape(nb, MOBA_BLOCK, 2 * LANES), axis=1))

    def scores(i, h):
        rows = slice(i * MOBA_BLOCK, (i + 1) * MOBA_BLOCK)
        if i == 0:
            qa = jnp.concatenate([q[h][0][rows, :], q[h][1][rows, :]], axis=1)
        else:
            qa = q_aug[h][rows, :]
        return _dot_nt(qa, k_aug[h][:(i + 1) * MOBA_BLOCK, :])

    def finish(i, h, s):
        s_diag = jnp.where(causal, s[:, i * MOBA_BLOCK:], NEG)
        s = s_diag if i == 0 else jnp.concatenate([s[:, :i * MOBA_BLOCK], s_diag], axis=1)
        m = jnp.max(s, axis=-1, keepdims=True)
        p = jnp.exp2(s - m).astype(BF16)
        o = _dot(p, v_aug[h][:(i + 1) * MOBA_BLOCK, :])
        o_ref[i * MOBA_BLOCK:(i + 1) * MOBA_BLOCK, h * A_HEAD_DIM:(h + 1) * A_HEAD_DIM] = (
            o[:, :A_HEAD_DIM] / o[:, A_HEAD_DIM:A_HEAD_DIM + 1]).astype(BF16)

    tasks = [(i, h) for i in range(nb) for h in range(heads)]
    pending = [scores(0, h) for h in range(heads)]
    for h in range(heads):
        bias = _moba_bias(jnp.concatenate(q[h], axis=1), kmeans[h // 2], seq, offsets[h])
        q_aug.append(jnp.concatenate([q[h][0] + bias, q[h][1]], axis=1))
    for t, (i, h) in enumerate(tasks):
        if t + heads < len(tasks):
            pending.append(scores(*tasks[t + heads]))
        finish(i, h, pending[t])


MOBA_HEADS_PER_STEP = 4


def _moba_attn(q, k, v, batch, seq):
    hps = MOBA_HEADS_PER_STEP
    spec = pl.BlockSpec((seq, hps * A_HEAD_DIM), lambda b, h: (b, h))
    return pl.pallas_call(
        functools.partial(_moba_attn_kernel, seq=seq, heads=hps),
        grid=(batch, A_HEADS // hps),
        in_specs=[spec, spec, spec],
        out_specs=spec,
        out_shape=jax.ShapeDtypeStruct(q.shape, BF16),
        compiler_params=pltpu.CompilerParams(dimension_semantics=("arbitrary", "arbitrary"),
                                             vmem_limit_bytes=VMEM_LIMIT),
        name="moba_attn",
    )(q, k, v)


WEIGHT_CAST_STEPS = 8


def _cast_slab(src_ref, dst_ref, step):
    rows = src_ref.shape[0]
    dst_ref[pl.ds(pl.multiple_of(step * rows, rows), rows), :] = src_ref[...].astype(BF16)


def _slab_spec(shape):
    rows = shape[0] // WEIGHT_CAST_STEPS
    return pl.BlockSpec((rows, shape[1]), lambda s: (jnp.minimum(s, WEIGHT_CAST_STEPS - 1), 0))


def _row_spec(cols):
    return pl.BlockSpec((ROW_TILE, cols), lambda s: (jnp.maximum(s - WEIGHT_CAST_STEPS, 0), 0))


def _mix_mlp_kernel(*refs, gated):
    if gated:
        a_ref, gate_ref, wo_f_ref, x_ref, g_ref, w1_f_ref, w2_f_ref, o_ref, wo_ref, w1_ref, w2_ref, u_ref = refs
    else:
        a_ref, wo_f_ref, x_ref, g_ref, w1_f_ref, w2_f_ref, o_ref, wo_ref, w1_ref, w2_ref, u_ref = refs
    step = pl.program_id(0)

    @pl.when(step < WEIGHT_CAST_STEPS)
    def _cast():
        _cast_slab(wo_f_ref, wo_ref, step)
        _cast_slab(w1_f_ref, w1_ref, step)
        _cast_slab(w2_f_ref, w2_ref, step)

    @pl.when(step >= WEIGHT_CAST_STEPS)
    def _compute():
        a = a_ref[...]
        if gated:
            a = (a.astype(F32) * gate_ref[...].astype(F32)).astype(BF16)
        x1 = x_ref[...] + _dot(a, wo_ref[...])
        h = _rms(x1, g_ref[...]).astype(BF16)
        for c in range(D_FF // D_MODEL):
            cols = slice(c * D_MODEL, (c + 1) * D_MODEL)
            u = jnp.maximum(_dot(h, w1_ref[:, cols]), 0.0)
            u_ref[:, cols] = (u * u).astype(BF16)
        o_ref[...] = x1 + _dot(u_ref[...], w2_ref[...])


def _mix_mlp(a, gate, wo, x2, g, w1, w2):
    t, kdim = a.shape
    gated = gate is not None
    acts = [a, gate] if gated else [a]
    return pl.pallas_call(
        functools.partial(_mix_mlp_kernel, gated=gated),
        grid=(WEIGHT_CAST_STEPS + t // ROW_TILE,),
        in_specs=[_row_spec(kdim)] * len(acts) + [
            _slab_spec(wo.shape), _row_spec(D_MODEL), _resident((1, D_MODEL)),
            _slab_spec(w1.shape), _slab_spec(w2.shape)],
        out_specs=_row_spec(D_MODEL),
        out_shape=jax.ShapeDtypeStruct(x2.shape, F32),
        scratch_shapes=[pltpu.VMEM(wo.shape, BF16), pltpu.VMEM(w1.shape, BF16), pltpu.VMEM(w2.shape, BF16),
                        pltpu.VMEM((ROW_TILE, D_FF), BF16)],
        compiler_params=_params(),
        name="mix_mlp",
    )(*acts, wo, x2, g, w1, w2)


RET_PROJ_CHUNK = 512


def _ret_proj_kernel(x_ref, g_ref, w_f_ref, cq_ref, sq_ref, ck_ref, sk_ref, q_ref, k_ref, v_ref, sg_ref, w_ref):
    step = pl.program_id(0)
    half = R_QK_DIM // 2
    chunk = RET_PROJ_CHUNK
    n_chunks = 6 * D_MODEL // chunk

    @pl.when(step < WEIGHT_CAST_STEPS)
    def _cast():
        _cast_slab(w_f_ref, w_ref, step)

    def rope_store(y, lo, out_ref, cos_ref, sin_ref):
        for j in range(chunk // R_QK_DIM):
            hh = lo // R_QK_DIM + j
            cos = cos_ref[hh]
            sin = sin_ref[hh]
            x1 = y[:, j * R_QK_DIM:j * R_QK_DIM + half]
            x2 = y[:, j * R_QK_DIM + half:(j + 1) * R_QK_DIM]
            out_ref[:, hh * R_QK_DIM:hh * R_QK_DIM + half] = (x1 * cos - x2 * sin).astype(BF16)
            out_ref[:, hh * R_QK_DIM + half:(hh + 1) * R_QK_DIM] = (x2 * cos + x1 * sin).astype(BF16)

    def epilogue(c, y):
        col = c * chunk
        if col < D_MODEL:
            rope_store(y, col, q_ref, cq_ref, sq_ref)
        elif col < 2 * D_MODEL:
            rope_store(y, col - D_MODEL, k_ref, ck_ref, sk_ref)
        elif col < 4 * D_MODEL:
            v_ref[:, col - 2 * D_MODEL:col - 2 * D_MODEL + chunk] = y.astype(BF16)
        else:
            sg_ref[:, col - 4 * D_MODEL:col - 4 * D_MODEL + chunk] = (y * jax.nn.sigmoid(y)).astype(BF16)

    @pl.when(step >= WEIGHT_CAST_STEPS)
    def _compute():
        h = _rms(x_ref[...], g_ref[...]).astype(BF16)

        def project(c):
            return _dot(h, w_ref[:, c * chunk:(c + 1) * chunk])

        pending = project(0)
        for c in range(n_chunks):
            following = project(c + 1) if c + 1 < n_chunks else None
            epilogue(c, pending)
            pending = following


def _ret_proj(x2, g, w, tables, seq):
    t = x2.shape[0]
    tiles_per_seq = seq // ROW_TILE
    half = R_QK_DIM // 2
    table = pl.BlockSpec((R_HEADS, ROW_TILE, half),
                         lambda s: (0, jnp.maximum(s - WEIGHT_CAST_STEPS, 0) % tiles_per_seq, 0))
    return pl.pallas_call(
        _ret_proj_kernel,
        grid=(WEIGHT_CAST_STEPS + t // ROW_TILE,),
        in_specs=[_row_spec(D_MODEL), _resident((1, D_MODEL)), _slab_spec(w.shape), table, table, table, table],
        out_specs=[_row_spec(D_MODEL), _row_spec(D_MODEL), _row_spec(2 * D_MODEL), _row_spec(2 * D_MODEL)],
        out_shape=[jax.ShapeDtypeStruct((t, D_MODEL), BF16), jax.ShapeDtypeStruct((t, D_MODEL), BF16),
                   jax.ShapeDtypeStruct((t, 2 * D_MODEL), BF16), jax.ShapeDtypeStruct((t, 2 * D_MODEL), BF16)],
        scratch_shapes=[pltpu.VMEM(w.shape, BF16)],
        compiler_params=_params(),
        name="ret_proj",
    )(x2, g, w, *tables)


RET_CHUNK = 256
RET_HEADS_PER_STEP = 2


def _retention_kernel(q_ref, k_ref, v_ref, gch_ref, o_ref, *, seq, heads):
    c = RET_CHUNK
    n_chunks = seq // c
    ti = lax.broadcasted_iota(jnp.int32, (c, c), 0)
    ui = lax.broadcasted_iota(jnp.int32, (c, c), 1)
    causal = ui <= ti

    def chunk_matmuls(hh, n):
        rows = slice(n * c, (n + 1) * c)
        q = q_ref[rows, hh * R_QK_DIM:(hh + 1) * R_QK_DIM]
        k = k_ref[rows, hh * R_QK_DIM:(hh + 1) * R_QK_DIM]
        v = v_ref[rows, hh * R_V_DIM:(hh + 1) * R_V_DIM]
        return _dot_nt(q, k), _dot_tn(k, v)

    tasks = [(hh, n) for hh in range(heads) for n in range(n_chunks)]
    pending = chunk_matmuls(*tasks[0])
    decayed = None
    for t, (hh, n) in enumerate(tasks):
        rows = slice(n * c, (n + 1) * c)
        vcols = slice(hh * R_V_DIM, (hh + 1) * R_V_DIM)
        following = chunk_matmuls(*tasks[t + 1]) if t + 1 < len(tasks) else None
        s, update = pending
        y = _dot(jnp.where(causal, s, 0.0).astype(BF16), v_ref[rows, vcols])
        if n > 0:
            y = y + _dot(q_ref[rows, hh * R_QK_DIM:(hh + 1) * R_QK_DIM], decayed.astype(BF16))
        if n + 1 < n_chunks:
            decayed = (update if n == 0 else decayed + update) * gch_ref[hh]
        o_ref[rows, vcols] = (y * lax.rsqrt(jnp.mean(y * y, axis=-1, keepdims=True) + EPS)).astype(BF16)
        pending = following


def _retention(q, k, v, gch, batch, seq):
    hps = RET_HEADS_PER_STEP
    qk_spec = pl.BlockSpec((seq, hps * R_QK_DIM), lambda b, h: (b, h))
    v_spec = pl.BlockSpec((seq, hps * R_V_DIM), lambda b, h: (b, h))
    return pl.pallas_call(
        functools.partial(_retention_kernel, seq=seq, heads=hps),
        grid=(batch, R_HEADS // hps),
        in_specs=[qk_spec, qk_spec, v_spec, pl.BlockSpec((hps, 1, 1), lambda b, h: (h, 0, 0))],
        out_specs=v_spec,
        out_shape=jax.ShapeDtypeStruct(v.shape, BF16),
        compiler_params=pltpu.CompilerParams(dimension_semantics=("arbitrary", "arbitrary"),
                                             vmem_limit_bytes=VMEM_LIMIT),
        name="retention",
    )(q, k, v, gch)


def _rope_tables(seq, half):
    inv = ROPE_THETA ** (-jnp.arange(half, dtype=F32) / half)
    ang = jnp.arange(seq, dtype=jnp.int32).astype(F32)[:, None] * inv[None, :]
    return jnp.cos(ang), jnp.sin(ang)


def _retention_tables(seq):
    cos, sin = _rope_tables(seq, R_QK_DIM // 2)
    log_g = jnp.log1p(-(2.0 ** (-5.0 - jnp.arange(R_HEADS, dtype=F32))))
    local = (jnp.arange(seq, dtype=jnp.int32) % RET_CHUNK).astype(F32)
    dq = jnp.exp(log_g[:, None] * local[None, :])[:, :, None]
    dk = jnp.exp(-log_g[:, None] * local[None, :])[:, :, None] * (R_QK_DIM ** -0.5)
    gch = jnp.exp(log_g * RET_CHUNK)[:, None, None]
    return (cos[None] * dq, sin[None] * dq, cos[None] * dk, sin[None] * dk), gch


def kernel(x, norm_mix_g, norm_mlp_g, a_w_qkv, a_q_gain, a_k_gain, a_w_o, r_w_in, r_w_out, mlp_w1, mlp_w2):
    batch, seq, d = x.shape
    assert d == D_MODEL and seq % ROW_TILE == 0 and seq % MOBA_BLOCK == 0 and seq % RET_CHUNK == 0
    depth = norm_mix_g.shape[0]
    x2 = x.reshape(batch * seq, d)

    cos_a, sin_a = _rope_tables(seq, A_HEAD_DIM // 2)
    cos_a = jnp.concatenate([cos_a, cos_a], axis=1)
    sin_a = jnp.concatenate([sin_a, sin_a], axis=1)
    ret_tables, gch = _retention_tables(seq)

    def pair_gain(gain):
        halves = gain.reshape(2, A_HEAD_DIM // 2)
        return jnp.concatenate([halves, halves], axis=1)

    for i in range(depth):
        j = i // 2
        g_mix = norm_mix_g[i][None, :]
        if i % 2 == 0:
            w_q, w_k, w_v = jnp.split(a_w_qkv[j], 3, axis=1)
            w_qkv = jnp.concatenate([_pair_layout(w_q), _pair_layout(w_k), w_v], axis=1).astype(BF16)
            q, k, v = _moba_qkv(x2, g_mix, w_qkv, pair_gain(a_q_gain[j]), pair_gain(a_k_gain[j]),
                                cos_a, sin_a, seq)
            a, gate, w_o = _moba_attn(q, k, v, batch, seq), None, a_w_o[j]
        else:
            q, k, v, gate = _ret_proj(x2, g_mix, r_w_in[j], ret_tables, seq)
            a, w_o = _retention(q, k, v, gch, batch, seq), r_w_out[j]
        x2 = _mix_mlp(a, gate, w_o, x2, norm_mlp_g[i][None, :], mlp_w1[i], mlp_w2[i])
    return x2.reshape(batch, seq, d)
```

```python
import functools

import jax
import jax.numpy as jnp
import numpy as np
from jax import lax
from jax.experimental import pallas as pl
from jax.experimental.pallas import tpu as pltpu

D_MODEL = 1024
A_HEADS = 8
A_HEAD_DIM = D_MODEL // A_HEADS
MOBA_BLOCK = 256
MOBA_TOPK = 3
R_HEADS = 4
R_QK_DIM = D_MODEL // R_HEADS
R_V_DIM = 2 * D_MODEL // R_HEADS
D_FF = 4 * D_MODEL
ROPE_THETA = 10000.0
EPS = 1e-6
NEG = -1e30

LANES = 128
ROW_TILE = 512
VMEM_LIMIT = 56 * 1024 * 1024

BF16 = jnp.bfloat16
F32 = jnp.float32


def _dot(a, b):
    return jnp.dot(a, b, preferred_element_type=F32)


def _dot_nt(a, b):
    return lax.dot_general(a, b, (((1,), (1,)), ((), ())), preferred_element_type=F32)


def _dot_tn(a, b):
    return lax.dot_general(a, b, (((0,), (0,)), ((), ())), preferred_element_type=F32)


def _rms(x, g):
    return x * lax.rsqrt(jnp.mean(x * x, axis=-1, keepdims=True) + EPS) * g


def _params():
    return pltpu.CompilerParams(dimension_semantics=("arbitrary",), vmem_limit_bytes=VMEM_LIMIT)


def _resident(shape):
    nd = len(shape)
    return pl.BlockSpec(shape, lambda *_: (0,) * nd, pipeline_mode=pl.Buffered(1))


def _pair_layout(w):
    lead = w.shape[:-1]
    half = A_HEAD_DIM // 2
    w = w.reshape(*lead, A_HEADS // 2, 2, 2, half)
    return jnp.swapaxes(w, -2, -3).reshape(*lead, D_MODEL)


def _moba_qkv_kernel(x_ref, g_ref, w_ref, qg_ref, kg_ref, cos_ref, sin_ref, q_ref, k_ref, v_ref):
    h = _rms(x_ref[...], g_ref[...]).astype(BF16)
    cos = cos_ref[...]
    sin = sin_ref[...]
    first = lax.broadcasted_iota(jnp.int32, (1, LANES), 1) < A_HEAD_DIM // 2

    def rope_tables(gain_ref, scale):
        g1 = gain_ref[0:1, :] * scale
        g2 = gain_ref[1:2, :] * scale
        return cos * g1, sin * g2, cos * g2, sin * g1

    tables = (rope_tables(qg_ref, A_HEAD_DIM ** -0.5 * float(np.log2(np.e))), rope_tables(kg_ref, 1.0))
    outs = (q_ref, k_ref, v_ref)
    chunk = 2 * A_HEAD_DIM
    n_chunks = 3 * D_MODEL // chunk

    def project(c):
        return _dot(h, w_ref[:, c * chunk:(c + 1) * chunk])

    def head_rsqrt(t, lanes):
        return lax.rsqrt(jnp.sum(jnp.where(lanes, t, 0.0), axis=-1, keepdims=True) * (1.0 / A_HEAD_DIM) + EPS)

    def epilogue(c, y):
        section, lo = divmod(c * chunk, D_MODEL)
        if section == 2:
            v_ref[:, lo:lo + chunk] = y.astype(BF16)
            return
        c1, s2, c2, s1 = tables[section]
        y1 = y[:, :LANES]
        y2 = y[:, LANES:]
        t = y1 * y1 + y2 * y2
        r = jnp.where(first, head_rsqrt(t, first), head_rsqrt(t, ~first))
        outs[section][:, lo:lo + LANES] = ((y1 * c1 - y2 * s2) * r).astype(BF16)
        outs[section][:, lo + LANES:lo + chunk] = ((y2 * c2 + y1 * s1) * r).astype(BF16)

    pending = project(0)
    for c in range(n_chunks):
        following = project(c + 1) if c + 1 < n_chunks else None
        epilogue(c, pending)
        pending = following


def _moba_qkv(x2, g, g_layer, w, qg, kg, cos, sin, batch, seq):
    t = x2.shape[0]
    tiles_per_seq = seq // ROW_TILE
    row = lambda i: ((i % batch) * tiles_per_seq + i // batch, 0)
    pos = lambda i: (i // batch, 0)
    out = jax.ShapeDtypeStruct((t, D_MODEL), BF16)
    return pl.pallas_call(
        _moba_qkv_kernel,
        grid=(t // ROW_TILE,),
        in_specs=[
            pl.BlockSpec((ROW_TILE, D_MODEL), row),
            _layer_row_spec(D_MODEL, g_layer),
            _resident((D_MODEL, 3 * D_MODEL)),
            _resident((2, A_HEAD_DIM)),
            _resident((2, A_HEAD_DIM)),
            pl.BlockSpec((ROW_TILE, A_HEAD_DIM), pos),
            pl.BlockSpec((ROW_TILE, A_HEAD_DIM), pos),
        ],
        out_specs=[pl.BlockSpec((ROW_TILE, D_MODEL), row)] * 3,
        out_shape=[out, out, out],
        compiler_params=_params(),
        name="moba_qkv",
    )(x2, g, w, qg, kg, cos, sin)


def _pair_gates(q_pair, k_pair, seq):
    nb = seq // MOBA_BLOCK
    kmean = jnp.mean(k_pair.astype(F32).reshape(nb, MOBA_BLOCK, 2 * LANES), axis=1)
    first = lax.broadcasted_iota(jnp.int32, (nb, 2 * LANES), 1) % LANES < A_HEAD_DIM // 2
    terms = []
    for own in (first, ~first):
        km = jnp.where(own, kmean, 0.0)
        hi = km.astype(BF16)
        terms += [hi, (km - hi.astype(F32)).astype(BF16)]
    g = _dot_nt(jnp.concatenate(terms, axis=0), q_pair)
    return g[0:nb] + g[nb:2 * nb], g[2 * nb:3 * nb] + g[3 * nb:4 * nb]


def _moba_bias(gate, seq, lane_offset):
    nb = seq // MOBA_BLOCK
    blk = lax.broadcasted_iota(jnp.int32, (nb, seq), 0)
    own = lax.broadcasted_iota(jnp.int32, (nb, seq), 1) // MOBA_BLOCK
    gate = jnp.where(blk < own, gate, NEG)
    rank = jnp.zeros((nb, seq), jnp.int32)
    for j in range(nb):
        gj = gate[j:j + 1, :]
        beats = (gj > gate) | ((gj == gate) & (blk > j))
        rank = rank + beats.astype(jnp.int32)
    chosen = ((rank < MOBA_TOPK) & (blk < own)) | (blk == own)
    bias_t = jnp.where(chosen, 0.0, NEG).astype(F32)
    pad = [jnp.zeros((rows, seq), F32) for rows in (lane_offset, LANES - nb - lane_offset)]
    parts = [part for part in (pad[0], bias_t, pad[1]) if part.shape[0]]
    return jnp.concatenate(parts, axis=0).T.astype(BF16)


def _moba_attn_kernel(q_ref, k_ref, v_ref, o_ref, *, seq, heads):
    nb = seq // MOBA_BLOCK
    half = A_HEAD_DIM // 2
    lane = lax.broadcasted_iota(jnp.int32, (seq, LANES), 1)
    key_block = lax.broadcasted_iota(jnp.int32, (seq, LANES), 0) // MOBA_BLOCK
    first = lane < half
    ones = jnp.ones((seq, LANES), BF16)
    qi_idx = lax.broadcasted_iota(jnp.int32, (MOBA_BLOCK, MOBA_BLOCK), 0)
    ki_idx = lax.broadcasted_iota(jnp.int32, (MOBA_BLOCK, MOBA_BLOCK), 1)
    causal = ki_idx <= qi_idx

    q, k_aug, v_aug, q_aug, offsets = [], [], [], [], []
    for h in range(heads):
        pair, second = divmod(h, 2)
        lo = pair * 2 * A_HEAD_DIM
        own = ~first if second else first
        offset = 0 if second else half
        bias_lanes = (lane >= offset) & (lane < offset + nb)
        onehot = (lane == key_block + offset).astype(BF16)
        zero = jnp.zeros((seq, LANES), BF16)
        q1 = jnp.where(own, q_ref[:, lo:lo + LANES], zero)
        q2 = jnp.where(own, q_ref[:, lo + LANES:lo + 2 * LANES], zero)
        q.append((q1, q2))
        k_aug.append(jnp.concatenate([jnp.where(bias_lanes, onehot, k_ref[:, lo:lo + LANES]),
                                      k_ref[:, lo + LANES:lo + 2 * LANES]], axis=1))
        v_aug.append(jnp.concatenate([v_ref[:, h * A_HEAD_DIM:(h + 1) * A_HEAD_DIM], ones], axis=1))
        offsets.append(offset)

    def scores(i, h):
        rows = slice(i * MOBA_BLOCK, (i + 1) * MOBA_BLOCK)
        if i == 0:
            qa = jnp.concatenate([q[h][0][rows, :], q[h][1][rows, :]], axis=1)
        else:
            qa = q_aug[h][rows, :]
        return _dot_nt(qa, k_aug[h][:(i + 1) * MOBA_BLOCK, :])

    def finish(i, h, s):
        s_diag = jnp.where(causal, s[:, i * MOBA_BLOCK:], NEG)
        s = s_diag if i == 0 else jnp.concatenate([s[:, :i * MOBA_BLOCK], s_diag], axis=1)
        m = jnp.max(s, axis=-1, keepdims=True)
        p = jnp.exp2(s - m).astype(BF16)
        o = _dot(p, v_aug[h][:(i + 1) * MOBA_BLOCK, :])
        o_ref[i * MOBA_BLOCK:(i + 1) * MOBA_BLOCK, h * A_HEAD_DIM:(h + 1) * A_HEAD_DIM] = (
            o[:, :A_HEAD_DIM] / o[:, A_HEAD_DIM:A_HEAD_DIM + 1]).astype(BF16)

    tasks = [(i, h) for i in range(nb) for h in range(heads)]
    pending = [scores(0, h) for h in range(heads)]
    for pair in range(heads // 2):
        cols = slice(pair * 2 * A_HEAD_DIM, (pair + 1) * 2 * A_HEAD_DIM)
        gates = _pair_gates(q_ref[:, cols], k_ref[:, cols], seq)
        for h in (2 * pair, 2 * pair + 1):
            bias = _moba_bias(gates[h % 2], seq, offsets[h])
            q_aug.append(jnp.concatenate([q[h][0] + bias, q[h][1]], axis=1))
    for t, (i, h) in enumerate(tasks):
        if t + heads < len(tasks):
            pending.append(scores(*tasks[t + heads]))
        finish(i, h, pending[t])


MOBA_HEADS_PER_STEP = 4


def _moba_attn(q, k, v, batch, seq):
    hps = MOBA_HEADS_PER_STEP
    spec = pl.BlockSpec((seq, hps * A_HEAD_DIM), lambda b, h: (b, h))
    return pl.pallas_call(
        functools.partial(_moba_attn_kernel, seq=seq, heads=hps),
        grid=(batch, A_HEADS // hps),
        in_specs=[spec, spec, spec],
        out_specs=spec,
        out_shape=jax.ShapeDtypeStruct(q.shape, BF16),
        compiler_params=pltpu.CompilerParams(dimension_semantics=("arbitrary", "arbitrary"),
                                             vmem_limit_bytes=VMEM_LIMIT),
        name="moba_attn",
    )(q, k, v)


WEIGHT_CAST_STEPS = 8


def _cast_slab(src_ref, dst_ref, step):
    rows = src_ref.shape[0]
    dst_ref[pl.ds(pl.multiple_of(step * rows, rows), rows), :] = src_ref[...].astype(BF16)


def _slab_spec(stacked, layer):
    _, rows, cols = stacked.shape
    return pl.BlockSpec((None, rows // WEIGHT_CAST_STEPS, cols),
                        lambda s: (layer, jnp.minimum(s, WEIGHT_CAST_STEPS - 1), 0))


def _layer_row_spec(cols, layer):
    return pl.BlockSpec((None, 1, cols), lambda *_: (layer, 0, 0), pipeline_mode=pl.Buffered(1))


def _row_spec(cols):
    return pl.BlockSpec((ROW_TILE, cols), lambda s: (jnp.maximum(s - WEIGHT_CAST_STEPS, 0), 0))


def _mix_mlp_kernel(*refs, gated):
    if gated:
        a_ref, gate_ref, wo_f_ref, x_ref, g_ref, w1_f_ref, w2_f_ref, o_ref, wo_ref, w1_ref, w2_ref, u_ref = refs
    else:
        a_ref, wo_f_ref, x_ref, g_ref, w1_f_ref, w2_f_ref, o_ref, wo_ref, w1_ref, w2_ref, u_ref = refs
    step = pl.program_id(0)

    @pl.when(step < WEIGHT_CAST_STEPS)
    def _cast():
        _cast_slab(wo_f_ref, wo_ref, step)
        _cast_slab(w1_f_ref, w1_ref, step)
        _cast_slab(w2_f_ref, w2_ref, step)

    @pl.when(step >= WEIGHT_CAST_STEPS)
    def _compute():
        a = a_ref[...]
        if gated:
            a = (a.astype(F32) * gate_ref[...].astype(F32)).astype(BF16)
        x1 = x_ref[...] + _dot(a, wo_ref[...])
        h = _rms(x1, g_ref[...]).astype(BF16)
        for c in range(D_FF // D_MODEL):
            cols = slice(c * D_MODEL, (c + 1) * D_MODEL)
            u = jnp.maximum(_dot(h, w1_ref[:, cols]), 0.0)
            u_ref[:, cols] = (u * u).astype(BF16)
        o_ref[...] = x1 + _dot(u_ref[...], w2_ref[...])


def _mix_mlp(a, gate, wo, wo_layer, x2, g, w1, w2, layer):
    t, kdim = a.shape
    gated = gate is not None
    acts = [a, gate] if gated else [a]
    return pl.pallas_call(
        functools.partial(_mix_mlp_kernel, gated=gated),
        grid=(WEIGHT_CAST_STEPS + t // ROW_TILE,),
        in_specs=[_row_spec(kdim)] * len(acts) + [
            _slab_spec(wo, wo_layer), _row_spec(D_MODEL), _layer_row_spec(D_MODEL, layer),
            _slab_spec(w1, layer), _slab_spec(w2, layer)],
        out_specs=_row_spec(D_MODEL),
        out_shape=jax.ShapeDtypeStruct(x2.shape, F32),
        scratch_shapes=[pltpu.VMEM(wo.shape[1:], BF16), pltpu.VMEM(w1.shape[1:], BF16),
                        pltpu.VMEM(w2.shape[1:], BF16), pltpu.VMEM((ROW_TILE, D_FF), BF16)],
        compiler_params=_params(),
        name="mix_mlp",
    )(*acts, wo, x2, g, w1, w2)


RET_PROJ_CHUNK = 512


def _ret_proj_kernel(x_ref, g_ref, w_f_ref, cq_ref, sq_ref, ck_ref, sk_ref, q_ref, k_ref, v_ref, sg_ref, w_ref):
    step = pl.program_id(0)
    half = R_QK_DIM // 2
    chunk = RET_PROJ_CHUNK
    n_chunks = 6 * D_MODEL // chunk

    @pl.when(step < WEIGHT_CAST_STEPS)
    def _cast():
        _cast_slab(w_f_ref, w_ref, step)

    def rope_store(y, lo, out_ref, cos_ref, sin_ref):
        for j in range(chunk // R_QK_DIM):
            hh = lo // R_QK_DIM + j
            cos = cos_ref[hh]
            sin = sin_ref[hh]
            x1 = y[:, j * R_QK_DIM:j * R_QK_DIM + half]
            x2 = y[:, j * R_QK_DIM + half:(j + 1) * R_QK_DIM]
            out_ref[:, hh * R_QK_DIM:hh * R_QK_DIM + half] = (x1 * cos - x2 * sin).astype(BF16)
            out_ref[:, hh * R_QK_DIM + half:(hh + 1) * R_QK_DIM] = (x2 * cos + x1 * sin).astype(BF16)

    def epilogue(c, y):
        col = c * chunk
        if col < D_MODEL:
            rope_store(y, col, q_ref, cq_ref, sq_ref)
        elif col < 2 * D_MODEL:
            rope_store(y, col - D_MODEL, k_ref, ck_ref, sk_ref)
        elif col < 4 * D_MODEL:
            v_ref[:, col - 2 * D_MODEL:col - 2 * D_MODEL + chunk] = y.astype(BF16)
        else:
            sg_ref[:, col - 4 * D_MODEL:col - 4 * D_MODEL + chunk] = (y * jax.nn.sigmoid(y)).astype(BF16)

    @pl.when(step >= WEIGHT_CAST_STEPS)
    def _compute():
        h = _rms(x_ref[...], g_ref[...]).astype(BF16)

        def project(c):
            return _dot(h, w_ref[:, c * chunk:(c + 1) * chunk])

        pending = project(0)
        for c in range(n_chunks):
            following = project(c + 1) if c + 1 < n_chunks else None
            epilogue(c, pending)
            pending = following


def _position_major(batch, seq):
    tiles_per_seq = seq // ROW_TILE

    def split(s):
        p, b = jnp.divmod(jnp.maximum(s - WEIGHT_CAST_STEPS, 0), batch)
        return p, b * tiles_per_seq + p

    return split


def _ret_proj(x2, g, g_layer, w, w_layer, tables, batch, seq):
    t = x2.shape[0]
    half = R_QK_DIM // 2
    split = _position_major(batch, seq)
    rows = lambda cols: pl.BlockSpec((ROW_TILE, cols), lambda s: (split(s)[1], 0))
    table = pl.BlockSpec((R_HEADS, ROW_TILE, half), lambda s: (0, split(s)[0], 0))
    return pl.pallas_call(
        _ret_proj_kernel,
        grid=(WEIGHT_CAST_STEPS + t // ROW_TILE,),
        in_specs=[rows(D_MODEL), _layer_row_spec(D_MODEL, g_layer), _slab_spec(w, w_layer),
                  table, table, table, table],
        out_specs=[rows(D_MODEL), rows(D_MODEL), rows(2 * D_MODEL), rows(2 * D_MODEL)],
        out_shape=[jax.ShapeDtypeStruct((t, D_MODEL), BF16), jax.ShapeDtypeStruct((t, D_MODEL), BF16),
                   jax.ShapeDtypeStruct((t, 2 * D_MODEL), BF16), jax.ShapeDtypeStruct((t, 2 * D_MODEL), BF16)],
        scratch_shapes=[pltpu.VMEM(w.shape[1:], BF16)],
        compiler_params=_params(),
        name="ret_proj",
    )(x2, g, w, *tables)


RET_CHUNK = 256
RET_HEADS_PER_STEP = 2


def _retention_kernel(q_ref, k_ref, v_ref, gch_ref, o_ref, *, seq, heads):
    c = RET_CHUNK
    n_chunks = seq // c
    ti = lax.broadcasted_iota(jnp.int32, (c, c), 0)
    ui = lax.broadcasted_iota(jnp.int32, (c, c), 1)
    causal = ui <= ti

    def chunk_matmuls(hh, n):
        rows = slice(n * c, (n + 1) * c)
        q = q_ref[rows, hh * R_QK_DIM:(hh + 1) * R_QK_DIM]
        k = k_ref[rows, hh * R_QK_DIM:(hh + 1) * R_QK_DIM]
        v = v_ref[rows, hh * R_V_DIM:(hh + 1) * R_V_DIM]
        return _dot_nt(q, k), _dot_tn(k, v)

    tasks = [(hh, n) for hh in range(heads) for n in range(n_chunks)]
    pending = chunk_matmuls(*tasks[0])
    decayed = None
    for t, (hh, n) in enumerate(tasks):
        rows = slice(n * c, (n + 1) * c)
        vcols = slice(hh * R_V_DIM, (hh + 1) * R_V_DIM)
        following = chunk_matmuls(*tasks[t + 1]) if t + 1 < len(tasks) else None
        s, update = pending
        y = _dot(jnp.where(causal, s, 0.0).astype(BF16), v_ref[rows, vcols])
        if n > 0:
            y = y + _dot(q_ref[rows, hh * R_QK_DIM:(hh + 1) * R_QK_DIM], decayed.astype(BF16))
        if n + 1 < n_chunks:
            decayed = (update if n == 0 else decayed + update) * gch_ref[hh]
        o_ref[rows, vcols] = (y * lax.rsqrt(jnp.mean(y * y, axis=-1, keepdims=True) + EPS)).astype(BF16)
        pending = following


def _retention(q, k, v, gch, batch, seq):
    hps = RET_HEADS_PER_STEP
    qk_spec = pl.BlockSpec((seq, hps * R_QK_DIM), lambda b, h: (b, h))
    v_spec = pl.BlockSpec((seq, hps * R_V_DIM), lambda b, h: (b, h))
    return pl.pallas_call(
        functools.partial(_retention_kernel, seq=seq, heads=hps),
        grid=(batch, R_HEADS // hps),
        in_specs=[qk_spec, qk_spec, v_spec, pl.BlockSpec((hps, 1, 1), lambda b, h: (h, 0, 0))],
        out_specs=v_spec,
        out_shape=jax.ShapeDtypeStruct(v.shape, BF16),
        compiler_params=pltpu.CompilerParams(dimension_semantics=("arbitrary", "arbitrary"),
                                             vmem_limit_bytes=VMEM_LIMIT),
        name="retention",
    )(q, k, v, gch)


def _rope_tables(seq, half):
    inv = (ROPE_THETA ** (-np.arange(half, dtype=np.float32) / half)).astype(np.float32)
    ang = np.arange(seq, dtype=np.float32)[:, None] * inv[None, :]
    return np.cos(ang), np.sin(ang)


def _retention_tables(seq):
    cos, sin = _rope_tables(seq, R_QK_DIM // 2)
    log_g = np.log1p(-(2.0 ** (-5.0 - np.arange(R_HEADS, dtype=np.float32)))).astype(np.float32)
    local = (np.arange(seq) % RET_CHUNK).astype(np.float32)
    dq = np.exp(log_g[:, None] * local[None, :])[:, :, None]
    dk = np.exp(-log_g[:, None] * local[None, :])[:, :, None] * np.float32(R_QK_DIM ** -0.5)
    gch = np.exp(log_g * np.float32(RET_CHUNK))[:, None, None]
    tables = (cos[None] * dq, sin[None] * dq, cos[None] * dk, sin[None] * dk)
    return tuple(jnp.asarray(t, F32) for t in tables), jnp.asarray(gch, F32)


def kernel(x, norm_mix_g, norm_mlp_g, a_w_qkv, a_q_gain, a_k_gain, a_w_o, r_w_in, r_w_out, mlp_w1, mlp_w2):
    batch, seq, d = x.shape
    assert d == D_MODEL and seq % ROW_TILE == 0 and seq % MOBA_BLOCK == 0 and seq % RET_CHUNK == 0
    depth = norm_mix_g.shape[0]
    x2 = x.reshape(batch * seq, d)
    norm_mix_g = norm_mix_g.reshape(depth, 1, d)
    norm_mlp_g = norm_mlp_g.reshape(depth, 1, d)

    cos_a, sin_a = _rope_tables(seq, A_HEAD_DIM // 2)
    cos_a = jnp.asarray(np.concatenate([cos_a, cos_a], axis=1))
    sin_a = jnp.asarray(np.concatenate([sin_a, sin_a], axis=1))
    ret_tables, gch = _retention_tables(seq)

    def pair_gain(gain):
        halves = gain.reshape(2, A_HEAD_DIM // 2)
        return jnp.concatenate([halves, halves], axis=1)

    for i in range(depth):
        j = i // 2
        if i % 2 == 0:
            w_q, w_k, w_v = jnp.split(a_w_qkv[j], 3, axis=1)
            w_qkv = jnp.concatenate([_pair_layout(w_q), _pair_layout(w_k), w_v], axis=1).astype(BF16)
            q, k, v = _moba_qkv(x2, norm_mix_g, i, w_qkv, pair_gain(a_q_gain[j]), pair_gain(a_k_gain[j]),
                                cos_a, sin_a, batch, seq)
            a, gate, w_o = _moba_attn(q, k, v, batch, seq), None, a_w_o
        else:
            q, k, v, gate = _ret_proj(x2, norm_mix_g, i, r_w_in, j, ret_tables, batch, seq)
            a, w_o = _retention(q, k, v, gch, batch, seq), r_w_out
        x2 = _mix_mlp(a, gate, w_o, j, x2, norm_mlp_g, mlp_w1, mlp_w2, i)
    return x2.reshape(batch, seq, d)
```

```python
import functools

import jax
import jax.numpy as jnp
import numpy as np
from jax import lax
from jax.experimental import pallas as pl
from jax.experimental.pallas import tpu as pltpu

D_MODEL = 1024
A_HEADS = 8
A_HEAD_DIM = D_MODEL // A_HEADS
MOBA_BLOCK = 256
MOBA_TOPK = 3
R_HEADS = 4
R_QK_DIM = D_MODEL // R_HEADS
R_V_DIM = 2 * D_MODEL // R_HEADS
D_FF = 4 * D_MODEL
ROPE_THETA = 10000.0
EPS = 1e-6
NEG = -1e30

LANES = 128
ROW_TILE = 512
VMEM_LIMIT = 56 * 1024 * 1024

BF16 = jnp.bfloat16
F32 = jnp.float32


def _dot(a, b):
    return jnp.dot(a, b, preferred_element_type=F32)


def _dot_nt(a, b):
    return lax.dot_general(a, b, (((1,), (1,)), ((), ())), preferred_element_type=F32)


def _dot_tn(a, b):
    return lax.dot_general(a, b, (((0,), (0,)), ((), ())), preferred_element_type=F32)


def _rms(x, g):
    return x * lax.rsqrt(jnp.mean(x * x, axis=-1, keepdims=True) + EPS) * g


def _params():
    return pltpu.CompilerParams(dimension_semantics=("arbitrary",), vmem_limit_bytes=VMEM_LIMIT)


def _resident(shape):
    nd = len(shape)
    return pl.BlockSpec(shape, lambda *_: (0,) * nd, pipeline_mode=pl.Buffered(1))


def _pair_layout(w):
    lead = w.shape[:-1]
    half = A_HEAD_DIM // 2
    w = w.reshape(*lead, A_HEADS // 2, 2, 2, half)
    return jnp.swapaxes(w, -2, -3).reshape(*lead, D_MODEL)


def _moba_qkv_kernel(x_ref, g_ref, w_ref, qg_ref, kg_ref, cos_ref, sin_ref, q_ref, k_ref, v_ref):
    h = _rms(x_ref[...], g_ref[...]).astype(BF16)
    cos = cos_ref[...]
    sin = sin_ref[...]
    first = lax.broadcasted_iota(jnp.int32, (1, LANES), 1) < A_HEAD_DIM // 2

    def rope_tables(gain_ref, scale):
        g1 = gain_ref[0:1, :] * scale
        g2 = gain_ref[1:2, :] * scale
        return cos * g1, sin * g2, cos * g2, sin * g1

    tables = (rope_tables(qg_ref, A_HEAD_DIM ** -0.5 * float(np.log2(np.e))), rope_tables(kg_ref, 1.0))
    outs = (q_ref, k_ref, v_ref)
    chunk = 2 * A_HEAD_DIM
    n_chunks = 3 * D_MODEL // chunk

    def project(c):
        return _dot(h, w_ref[:, c * chunk:(c + 1) * chunk])

    def head_rsqrt(t, lanes):
        return lax.rsqrt(jnp.sum(jnp.where(lanes, t, 0.0), axis=-1, keepdims=True) * (1.0 / A_HEAD_DIM) + EPS)

    def epilogue(c, y):
        section, lo = divmod(c * chunk, D_MODEL)
        if section == 2:
            v_ref[:, lo:lo + chunk] = y.astype(BF16)
            return
        c1, s2, c2, s1 = tables[section]
        y1 = y[:, :LANES]
        y2 = y[:, LANES:]
        t = y1 * y1 + y2 * y2
        r = jnp.where(first, head_rsqrt(t, first), head_rsqrt(t, ~first))
        outs[section][:, lo:lo + LANES] = ((y1 * c1 - y2 * s2) * r).astype(BF16)
        outs[section][:, lo + LANES:lo + chunk] = ((y2 * c2 + y1 * s1) * r).astype(BF16)

    pending = project(0)
    for c in range(n_chunks):
        following = project(c + 1) if c + 1 < n_chunks else None
        epilogue(c, pending)
        pending = following


def _moba_qkv(x2, g, g_layer, w, qg, kg, cos, sin, batch, seq):
    t = x2.shape[0]
    tiles_per_seq = seq // ROW_TILE
    row = lambda i: ((i % batch) * tiles_per_seq + i // batch, 0)
    pos = lambda i: (i // batch, 0)
    out = jax.ShapeDtypeStruct((t, D_MODEL), BF16)
    return pl.pallas_call(
        _moba_qkv_kernel,
        grid=(t // ROW_TILE,),
        in_specs=[
            pl.BlockSpec((ROW_TILE, D_MODEL), row),
            _layer_row_spec(D_MODEL, g_layer),
            _resident((D_MODEL, 3 * D_MODEL)),
            _resident((2, A_HEAD_DIM)),
            _resident((2, A_HEAD_DIM)),
            pl.BlockSpec((ROW_TILE, A_HEAD_DIM), pos),
            pl.BlockSpec((ROW_TILE, A_HEAD_DIM), pos),
        ],
        out_specs=[pl.BlockSpec((ROW_TILE, D_MODEL), row)] * 3,
        out_shape=[out, out, out],
        compiler_params=_params(),
        name="moba_qkv",
    )(x2, g, w, qg, kg, cos, sin)


def _pair_gates(q_pair, k_pair, seq):
    nb = seq // MOBA_BLOCK
    kmean = jnp.mean(k_pair.astype(F32).reshape(nb, MOBA_BLOCK, 2 * LANES), axis=1)
    first = lax.broadcasted_iota(jnp.int32, (nb, 2 * LANES), 1) % LANES < A_HEAD_DIM // 2
    terms = []
    for own in (first, ~first):
        km = jnp.where(own, kmean, 0.0)
        hi = km.astype(BF16)
        terms += [hi, (km - hi.astype(F32)).astype(BF16)]
    g = _dot_nt(jnp.concatenate(terms, axis=0), q_pair)
    return g[0:nb] + g[nb:2 * nb], g[2 * nb:3 * nb] + g[3 * nb:4 * nb]


def _moba_bias(gate, seq, lane_offset):
    nb = seq // MOBA_BLOCK
    blk = lax.broadcasted_iota(jnp.int32, (nb, seq), 0)
    own = lax.broadcasted_iota(jnp.int32, (nb, seq), 1) // MOBA_BLOCK
    gate = jnp.where(blk < own, gate, NEG)
    rank = jnp.zeros((nb, seq), jnp.int32)
    for j in range(nb):
        gj = gate[j:j + 1, :]
        beats = (gj > gate) | ((gj == gate) & (blk > j))
        rank = rank + beats.astype(jnp.int32)
    chosen = ((rank < MOBA_TOPK) & (blk < own)) | (blk == own)
    bias_t = jnp.where(chosen, 0.0, NEG).astype(F32)
    pad = [jnp.zeros((rows, seq), F32) for rows in (lane_offset, LANES - nb - lane_offset)]
    parts = [part for part in (pad[0], bias_t, pad[1]) if part.shape[0]]
    return jnp.concatenate(parts, axis=0).T.astype(BF16)


def _moba_attn_kernel(q_ref, k_ref, v_ref, o_ref, *, seq, heads):
    nb = seq // MOBA_BLOCK
    half = A_HEAD_DIM // 2
    lane = lax.broadcasted_iota(jnp.int32, (seq, LANES), 1)
    key_block = lax.broadcasted_iota(jnp.int32, (seq, LANES), 0) // MOBA_BLOCK
    first = lane < half
    ones = jnp.ones((seq, LANES), BF16)
    qi_idx = lax.broadcasted_iota(jnp.int32, (MOBA_BLOCK, MOBA_BLOCK), 0)
    ki_idx = lax.broadcasted_iota(jnp.int32, (MOBA_BLOCK, MOBA_BLOCK), 1)
    causal = ki_idx <= qi_idx

    q, k_aug, v_aug, q_aug, offsets = [], [], [], [], []
    for h in range(heads):
        pair, second = divmod(h, 2)
        lo = pair * 2 * A_HEAD_DIM
        own = ~first if second else first
        offset = 0 if second else half
        bias_lanes = (lane >= offset) & (lane < offset + nb)
        onehot = (lane == key_block + offset).astype(BF16)
        zero = jnp.zeros((seq, LANES), BF16)
        q1 = jnp.where(own, q_ref[:, lo:lo + LANES], zero)
        q2 = jnp.where(own, q_ref[:, lo + LANES:lo + 2 * LANES], zero)
        q.append((q1, q2))
        k_aug.append(jnp.concatenate([jnp.where(bias_lanes, onehot, k_ref[:, lo:lo + LANES]),
                                      k_ref[:, lo + LANES:lo + 2 * LANES]], axis=1))
        v_aug.append(jnp.concatenate([v_ref[:, h * A_HEAD_DIM:(h + 1) * A_HEAD_DIM], ones], axis=1))
        offsets.append(offset)

    def scores(i, h):
        rows = slice(i * MOBA_BLOCK, (i + 1) * MOBA_BLOCK)
        if i == 0:
            qa = jnp.concatenate([q[h][0][rows, :], q[h][1][rows, :]], axis=1)
        else:
            qa = q_aug[h][rows, :]
        return _dot_nt(qa, k_aug[h][:(i + 1) * MOBA_BLOCK, :])

    def finish(i, h, s):
        s_diag = jnp.where(causal, s[:, i * MOBA_BLOCK:], NEG)
        s = s_diag if i == 0 else jnp.concatenate([s[:, :i * MOBA_BLOCK], s_diag], axis=1)
        m = jnp.max(s, axis=-1, keepdims=True)
        p = jnp.exp2(s - m).astype(BF16)
        o = _dot(p, v_aug[h][:(i + 1) * MOBA_BLOCK, :])
        o_ref[i * MOBA_BLOCK:(i + 1) * MOBA_BLOCK, h * A_HEAD_DIM:(h + 1) * A_HEAD_DIM] = (
            o[:, :A_HEAD_DIM] / o[:, A_HEAD_DIM:A_HEAD_DIM + 1]).astype(BF16)

    tasks = [(i, h) for i in range(nb) for h in range(heads)]
    pending = [scores(0, h) for h in range(heads)]
    for pair in range(heads // 2):
        cols = slice(pair * 2 * A_HEAD_DIM, (pair + 1) * 2 * A_HEAD_DIM)
        gates = _pair_gates(q_ref[:, cols], k_ref[:, cols], seq)
        for h in (2 * pair, 2 * pair + 1):
            bias = _moba_bias(gates[h % 2], seq, offsets[h])
            q_aug.append(jnp.concatenate([q[h][0] + bias, q[h][1]], axis=1))
    for t, (i, h) in enumerate(tasks):
        if t + heads < len(tasks):
            pending.append(scores(*tasks[t + heads]))
        finish(i, h, pending[t])


MOBA_HEADS_PER_STEP = 4


def _moba_attn(q, k, v, batch, seq):
    hps = MOBA_HEADS_PER_STEP
    spec = pl.BlockSpec((seq, hps * A_HEAD_DIM), lambda b, h: (b, h))
    return pl.pallas_call(
        functools.partial(_moba_attn_kernel, seq=seq, heads=hps),
        grid=(batch, A_HEADS // hps),
        in_specs=[spec, spec, spec],
        out_specs=spec,
        out_shape=jax.ShapeDtypeStruct(q.shape, BF16),
        compiler_params=pltpu.CompilerParams(dimension_semantics=("arbitrary", "arbitrary"),
                                             vmem_limit_bytes=VMEM_LIMIT),
        name="moba_attn",
    )(q, k, v)


WEIGHT_CAST_STEPS = 8


def _cast_slab(src_ref, dst_ref, step):
    rows = src_ref.shape[0]
    dst_ref[pl.ds(pl.multiple_of(step * rows, rows), rows), :] = src_ref[...].astype(BF16)


def _slab_spec(stacked, layer):
    _, rows, cols = stacked.shape
    return pl.BlockSpec((None, rows // WEIGHT_CAST_STEPS, cols),
                        lambda s: (layer, jnp.minimum(s, WEIGHT_CAST_STEPS - 1), 0))


def _layer_row_spec(cols, layer):
    return pl.BlockSpec((None, 1, cols), lambda *_: (layer, 0, 0), pipeline_mode=pl.Buffered(1))


def _row_spec(cols):
    return pl.BlockSpec((ROW_TILE, cols), lambda s: (jnp.maximum(s - WEIGHT_CAST_STEPS, 0), 0))


def _mix_mlp_kernel(*refs, gated):
    if gated:
        a_ref, gate_ref, wo_f_ref, x_ref, g_ref, w1_f_ref, w2_f_ref, o_ref, wo_ref, w1_ref, w2_ref, u_ref = refs
    else:
        a_ref, wo_f_ref, x_ref, g_ref, w1_f_ref, w2_f_ref, o_ref, wo_ref, w1_ref, w2_ref, u_ref = refs
    step = pl.program_id(0)

    @pl.when(step < WEIGHT_CAST_STEPS)
    def _cast():
        _cast_slab(wo_f_ref, wo_ref, step)
        _cast_slab(w1_f_ref, w1_ref, step)
        _cast_slab(w2_f_ref, w2_ref, step)

    @pl.when(step >= WEIGHT_CAST_STEPS)
    def _compute():
        a = a_ref[...]
        if gated:
            a = (a.astype(F32) * gate_ref[...].astype(F32)).astype(BF16)
        x1 = x_ref[...] + _dot(a, wo_ref[...])
        h = _rms(x1, g_ref[...]).astype(BF16)
        for c in range(D_FF // D_MODEL):
            cols = slice(c * D_MODEL, (c + 1) * D_MODEL)
            u = jnp.maximum(_dot(h, w1_ref[:, cols]), 0.0)
            u_ref[:, cols] = (u * u).astype(BF16)
        o_ref[...] = x1 + _dot(u_ref[...], w2_ref[...])


def _mix_mlp(a, gate, wo, wo_layer, x2, g, w1, w2, layer):
    t, kdim = a.shape
    gated = gate is not None
    acts = [a, gate] if gated else [a]
    return pl.pallas_call(
        functools.partial(_mix_mlp_kernel, gated=gated),
        grid=(WEIGHT_CAST_STEPS + t // ROW_TILE,),
        in_specs=[_row_spec(kdim)] * len(acts) + [
            _slab_spec(wo, wo_layer), _row_spec(D_MODEL), _layer_row_spec(D_MODEL, layer),
            _slab_spec(w1, layer), _slab_spec(w2, layer)],
        out_specs=_row_spec(D_MODEL),
        out_shape=jax.ShapeDtypeStruct(x2.shape, F32),
        scratch_shapes=[pltpu.VMEM(wo.shape[1:], BF16), pltpu.VMEM(w1.shape[1:], BF16),
                        pltpu.VMEM(w2.shape[1:], BF16), pltpu.VMEM((ROW_TILE, D_FF), BF16)],
        compiler_params=_params(),
        name="mix_mlp",
    )(*acts, wo, x2, g, w1, w2)


RET_PROJ_CHUNK = 512
RET_CHUNK = 256


def _ret_proj_kernel(x_ref, g_ref, w_f_ref, cos_ref, sin_ref, lg_ref, q_ref, k_ref, v_ref, sg_ref, w_ref):
    step = pl.program_id(0)
    half = R_QK_DIM // 2
    chunk = RET_PROJ_CHUNK
    n_chunks = 6 * D_MODEL // chunk

    @pl.when(step < WEIGHT_CAST_STEPS)
    def _cast():
        _cast_slab(w_f_ref, w_ref, step)

    def rope_store(y, lo, out_ref, sign, scale):
        local = (lax.broadcasted_iota(jnp.int32, (ROW_TILE, 1), 0) % RET_CHUNK).astype(F32)
        for j in range(chunk // R_QK_DIM):
            hh = lo // R_QK_DIM + j
            decay = jnp.exp(lg_ref[hh] * (sign * local)) * scale
            c, s = cos_ref[...] * decay, sin_ref[...] * decay
            x1 = y[:, j * R_QK_DIM:j * R_QK_DIM + half]
            x2 = y[:, j * R_QK_DIM + half:(j + 1) * R_QK_DIM]
            out_ref[:, hh * R_QK_DIM:hh * R_QK_DIM + half] = (x1 * c - x2 * s).astype(BF16)
            out_ref[:, hh * R_QK_DIM + half:(hh + 1) * R_QK_DIM] = (x2 * c + x1 * s).astype(BF16)

    def epilogue(c, y):
        col = c * chunk
        if col < D_MODEL:
            rope_store(y, col, q_ref, 1.0, 1.0)
        elif col < 2 * D_MODEL:
            rope_store(y, col - D_MODEL, k_ref, -1.0, R_QK_DIM ** -0.5)
        elif col < 4 * D_MODEL:
            v_ref[:, col - 2 * D_MODEL:col - 2 * D_MODEL + chunk] = y.astype(BF16)
        else:
            sg_ref[:, col - 4 * D_MODEL:col - 4 * D_MODEL + chunk] = (y * jax.nn.sigmoid(y)).astype(BF16)

    @pl.when(step >= WEIGHT_CAST_STEPS)
    def _compute():
        h = _rms(x_ref[...], g_ref[...]).astype(BF16)

        def project(c):
            return _dot(h, w_ref[:, c * chunk:(c + 1) * chunk])

        pending = project(0)
        for c in range(n_chunks):
            following = project(c + 1) if c + 1 < n_chunks else None
            epilogue(c, pending)
            pending = following


def _position_major(batch, seq):
    tiles_per_seq = seq // ROW_TILE

    def split(s):
        p, b = jnp.divmod(jnp.maximum(s - WEIGHT_CAST_STEPS, 0), batch)
        return p, b * tiles_per_seq + p

    return split


def _ret_proj(x2, g, g_layer, w, w_layer, cos, sin, log_g, batch, seq):
    t = x2.shape[0]
    half = R_QK_DIM // 2
    split = _position_major(batch, seq)
    rows = lambda cols: pl.BlockSpec((ROW_TILE, cols), lambda s: (split(s)[1], 0))
    table = pl.BlockSpec((ROW_TILE, half), lambda s: (split(s)[0], 0))
    return pl.pallas_call(
        _ret_proj_kernel,
        grid=(WEIGHT_CAST_STEPS + t // ROW_TILE,),
        in_specs=[rows(D_MODEL), _layer_row_spec(D_MODEL, g_layer), _slab_spec(w, w_layer),
                  table, table, _resident((R_HEADS, 1, 1))],
        out_specs=[rows(D_MODEL), rows(D_MODEL), rows(2 * D_MODEL), rows(2 * D_MODEL)],
        out_shape=[jax.ShapeDtypeStruct((t, D_MODEL), BF16), jax.ShapeDtypeStruct((t, D_MODEL), BF16),
                   jax.ShapeDtypeStruct((t, 2 * D_MODEL), BF16), jax.ShapeDtypeStruct((t, 2 * D_MODEL), BF16)],
        scratch_shapes=[pltpu.VMEM(w.shape[1:], BF16)],
        compiler_params=_params(),
        name="ret_proj",
    )(x2, g, w, cos, sin, log_g)


RET_HEADS_PER_STEP = 2


def _retention_kernel(q_ref, k_ref, v_ref, gch_ref, o_ref, *, seq, heads):
    c = RET_CHUNK
    n_chunks = seq // c
    ti = lax.broadcasted_iota(jnp.int32, (c, c), 0)
    ui = lax.broadcasted_iota(jnp.int32, (c, c), 1)
    causal = ui <= ti

    def chunk_matmuls(hh, n):
        rows = slice(n * c, (n + 1) * c)
        q = q_ref[rows, hh * R_QK_DIM:(hh + 1) * R_QK_DIM]
        k = k_ref[rows, hh * R_QK_DIM:(hh + 1) * R_QK_DIM]
        v = v_ref[rows, hh * R_V_DIM:(hh + 1) * R_V_DIM]
        return _dot_nt(q, k), _dot_tn(k, v)

    tasks = [(hh, n) for hh in range(heads) for n in range(n_chunks)]
    pending = chunk_matmuls(*tasks[0])
    decayed = None
    for t, (hh, n) in enumerate(tasks):
        rows = slice(n * c, (n + 1) * c)
        vcols = slice(hh * R_V_DIM, (hh + 1) * R_V_DIM)
        following = chunk_matmuls(*tasks[t + 1]) if t + 1 < len(tasks) else None
        s, update = pending
        y = _dot(jnp.where(causal, s, 0.0).astype(BF16), v_ref[rows, vcols])
        if n > 0:
            y = y + _dot(q_ref[rows, hh * R_QK_DIM:(hh + 1) * R_QK_DIM], decayed.astype(BF16))
        if n + 1 < n_chunks:
            decayed = (update if n == 0 else decayed + update) * gch_ref[hh]
        o_ref[rows, vcols] = (y * lax.rsqrt(jnp.mean(y * y, axis=-1, keepdims=True) + EPS)).astype(BF16)
        pending = following


def _retention(q, k, v, gch, batch, seq):
    hps = RET_HEADS_PER_STEP
    qk_spec = pl.BlockSpec((seq, hps * R_QK_DIM), lambda b, h: (b, h))
    v_spec = pl.BlockSpec((seq, hps * R_V_DIM), lambda b, h: (b, h))
    return pl.pallas_call(
        functools.partial(_retention_kernel, seq=seq, heads=hps),
        grid=(batch, R_HEADS // hps),
        in_specs=[qk_spec, qk_spec, v_spec, pl.BlockSpec((hps, 1, 1), lambda b, h: (h, 0, 0))],
        out_specs=v_spec,
        out_shape=jax.ShapeDtypeStruct(v.shape, BF16),
        compiler_params=pltpu.CompilerParams(dimension_semantics=("arbitrary", "arbitrary"),
                                             vmem_limit_bytes=VMEM_LIMIT),
        name="retention",
    )(q, k, v, gch)


def _rope_tables(seq, half):
    inv = ROPE_THETA ** (-jnp.arange(half, dtype=F32) / half)
    ang = jnp.arange(seq, dtype=jnp.int32).astype(F32)[:, None] * inv[None, :]
    return jnp.cos(ang), jnp.sin(ang)


def _retention_decay():
    log_g = jnp.log1p(-(2.0 ** (-5.0 - jnp.arange(R_HEADS, dtype=F32))))
    return log_g[:, None, None], jnp.exp(log_g * RET_CHUNK)[:, None, None]


def kernel(x, norm_mix_g, norm_mlp_g, a_w_qkv, a_q_gain, a_k_gain, a_w_o, r_w_in, r_w_out, mlp_w1, mlp_w2):
    batch, seq, d = x.shape
    assert d == D_MODEL and seq % ROW_TILE == 0 and seq % MOBA_BLOCK == 0 and ROW_TILE % RET_CHUNK == 0
    depth = norm_mix_g.shape[0]
    x2 = x.reshape(batch * seq, d)
    norm_mix_g = norm_mix_g.reshape(depth, 1, d)
    norm_mlp_g = norm_mlp_g.reshape(depth, 1, d)

    cos_a, sin_a = _rope_tables(seq, A_HEAD_DIM // 2)
    cos_a = jnp.concatenate([cos_a, cos_a], axis=1)
    sin_a = jnp.concatenate([sin_a, sin_a], axis=1)
    cos_r, sin_r = _rope_tables(seq, R_QK_DIM // 2)
    log_g, gch = _retention_decay()

    def pair_gain(gain):
        halves = gain.reshape(2, A_HEAD_DIM // 2)
        return jnp.concatenate([halves, halves], axis=1)

    for i in range(depth):
        j = i // 2
        if i % 2 == 0:
            w_q, w_k, w_v = jnp.split(a_w_qkv[j], 3, axis=1)
            w_qkv = jnp.concatenate([_pair_layout(w_q), _pair_layout(w_k), w_v], axis=1).astype(BF16)
            q, k, v = _moba_qkv(x2, norm_mix_g, i, w_qkv, pair_gain(a_q_gain[j]), pair_gain(a_k_gain[j]),
                                cos_a, sin_a, batch, seq)
            a, gate, w_o = _moba_attn(q, k, v, batch, seq), None, a_w_o
        else:
            q, k, v, gate = _ret_proj(x2, norm_mix_g, i, r_w_in, j, cos_r, sin_r, log_g, batch, seq)
            a, w_o = _retention(q, k, v, gch, batch, seq), r_w_out
        x2 = _mix_mlp(a, gate, w_o, j, x2, norm_mlp_g, mlp_w1, mlp_w2, i)
    return x2.reshape(batch, seq, d)
```

```python
import functools

import jax
import jax.numpy as jnp
import numpy as np
from jax import lax
from jax.experimental import pallas as pl
from jax.experimental.pallas import tpu as pltpu

D_MODEL = 1024
A_HEADS = 8
A_HEAD_DIM = D_MODEL // A_HEADS
MOBA_BLOCK = 256
MOBA_TOPK = 3
R_HEADS = 4
R_QK_DIM = D_MODEL // R_HEADS
R_V_DIM = 2 * D_MODEL // R_HEADS
D_FF = 4 * D_MODEL
ROPE_THETA = 10000.0
EPS = 1e-6
NEG = -1e30

LANES = 128
ROW_TILE = 512
VMEM_LIMIT = 56 * 1024 * 1024

BF16 = jnp.bfloat16
F32 = jnp.float32


def _dot(a, b):
    return jnp.dot(a, b, preferred_element_type=F32)


def _dot_nt(a, b):
    return lax.dot_general(a, b, (((1,), (1,)), ((), ())), preferred_element_type=F32)


def _dot_tn(a, b):
    return lax.dot_general(a, b, (((0,), (0,)), ((), ())), preferred_element_type=F32)


def _rms(x, g):
    return x * lax.rsqrt(jnp.mean(x * x, axis=-1, keepdims=True) + EPS) * g


def _params():
    return pltpu.CompilerParams(dimension_semantics=("arbitrary",), vmem_limit_bytes=VMEM_LIMIT)


def _resident(shape):
    nd = len(shape)
    return pl.BlockSpec(shape, lambda *_: (0,) * nd, pipeline_mode=pl.Buffered(1))


def _pair_halves(x, first):
    half = A_HEAD_DIM // 2
    c0, c1 = x[:, :LANES], x[:, LANES:]
    return (jnp.where(first, c0, pltpu.roll(c1, half, axis=1)),
            jnp.where(first, pltpu.roll(c0, half, axis=1), c1))


def _moba_qkv_kernel(x_ref, g_ref, w_f_ref, qg_ref, kg_ref, cos_ref, sin_ref, q_ref, k_ref, v_ref, w_ref):
    step = pl.program_id(0)
    chunk = 2 * A_HEAD_DIM
    n_chunks = 3 * D_MODEL // chunk

    @pl.when(step < WEIGHT_CAST_STEPS)
    def _cast():
        slab = w_f_ref.shape[0]
        rows = pl.ds(pl.multiple_of(step * slab, slab), slab)
        first = lax.broadcasted_iota(jnp.int32, (slab, LANES), 1) < A_HEAD_DIM // 2
        for c in range(n_chunks):
            cols = slice(c * chunk, (c + 1) * chunk)
            if c * chunk < 2 * D_MODEL:
                a, b = _pair_halves(w_f_ref[:, cols], first)
                w_ref[rows, c * chunk:c * chunk + LANES] = a.astype(BF16)
                w_ref[rows, c * chunk + LANES:(c + 1) * chunk] = b.astype(BF16)
            else:
                w_ref[rows, cols] = w_f_ref[:, cols].astype(BF16)

    @pl.when(step >= WEIGHT_CAST_STEPS)
    def _compute():
        h = _rms(x_ref[...], g_ref[...]).astype(BF16)
        cos = cos_ref[...]
        sin = sin_ref[...]
        first = lax.broadcasted_iota(jnp.int32, (1, LANES), 1) < A_HEAD_DIM // 2

        def rope_tables(gain_ref, scale):
            gain = jnp.broadcast_to(gain_ref[...], (8, A_HEAD_DIM)) * scale
            turned = pltpu.roll(gain, A_HEAD_DIM // 2, axis=1)
            g1 = jnp.where(first, gain, turned)[0:1]
            g2 = jnp.where(first, turned, gain)[0:1]
            return cos * g1, sin * g2, cos * g2, sin * g1

        tables = (rope_tables(qg_ref, A_HEAD_DIM ** -0.5 * float(np.log2(np.e))), rope_tables(kg_ref, 1.0))
        outs = (q_ref, k_ref, v_ref)

        def project(c):
            return _dot(h, w_ref[:, c * chunk:(c + 1) * chunk])

        def head_rsqrt(t, lanes):
            return lax.rsqrt(jnp.sum(jnp.where(lanes, t, 0.0), axis=-1, keepdims=True) * (1.0 / A_HEAD_DIM) + EPS)

        def epilogue(c, y):
            section, lo = divmod(c * chunk, D_MODEL)
            if section == 2:
                v_ref[:, lo:lo + chunk] = y.astype(BF16)
                return
            c1, s2, c2, s1 = tables[section]
            y1 = y[:, :LANES]
            y2 = y[:, LANES:]
            t = y1 * y1 + y2 * y2
            r = jnp.where(first, head_rsqrt(t, first), head_rsqrt(t, ~first))
            outs[section][:, lo:lo + LANES] = ((y1 * c1 - y2 * s2) * r).astype(BF16)
            outs[section][:, lo + LANES:lo + chunk] = ((y2 * c2 + y1 * s1) * r).astype(BF16)

        pending = project(0)
        for c in range(n_chunks):
            following = project(c + 1) if c + 1 < n_chunks else None
            epilogue(c, pending)
            pending = following


def _moba_qkv(x2, g, g_layer, w, w_layer, qg, kg, cos, sin, batch, seq):
    t = x2.shape[0]
    split = _position_major(batch, seq)
    rows = pl.BlockSpec((ROW_TILE, D_MODEL), lambda s: (split(s)[1], 0))
    table = pl.BlockSpec((ROW_TILE, A_HEAD_DIM), lambda s: (split(s)[0], 0))
    out = jax.ShapeDtypeStruct((t, D_MODEL), BF16)
    return pl.pallas_call(
        _moba_qkv_kernel,
        grid=(WEIGHT_CAST_STEPS + t // ROW_TILE,),
        in_specs=[rows, _layer_row_spec(D_MODEL, g_layer), _slab_spec(w, w_layer),
                  _layer_row_spec(A_HEAD_DIM, w_layer), _layer_row_spec(A_HEAD_DIM, w_layer), table, table],
        out_specs=[rows] * 3,
        out_shape=[out, out, out],
        scratch_shapes=[pltpu.VMEM(w.shape[1:], BF16)],
        compiler_params=_params(),
        name="moba_qkv",
    )(x2, g, w, qg, kg, cos, sin)


def _pair_gates(q_pair, k_pair, seq):
    nb = seq // MOBA_BLOCK
    kmean = jnp.mean(k_pair.astype(F32).reshape(nb, MOBA_BLOCK, 2 * LANES), axis=1)
    first = lax.broadcasted_iota(jnp.int32, (nb, 2 * LANES), 1) % LANES < A_HEAD_DIM // 2
    terms = []
    for own in (first, ~first):
        km = jnp.where(own, kmean, 0.0)
        hi = km.astype(BF16)
        terms += [hi, (km - hi.astype(F32)).astype(BF16)]
    g = _dot_nt(jnp.concatenate(terms, axis=0), q_pair)
    return g[0:nb] + g[nb:2 * nb], g[2 * nb:3 * nb] + g[3 * nb:4 * nb]


def _moba_bias(gate, seq, lane_offset):
    nb = seq // MOBA_BLOCK
    blk = lax.broadcasted_iota(jnp.int32, (nb, seq), 0)
    own = lax.broadcasted_iota(jnp.int32, (nb, seq), 1) // MOBA_BLOCK
    gate = jnp.where(blk < own, gate, NEG)
    rank = jnp.zeros((nb, seq), jnp.int32)
    for j in range(nb):
        gj = gate[j:j + 1, :]
        beats = (gj > gate) | ((gj == gate) & (blk > j))
        rank = rank + beats.astype(jnp.int32)
    chosen = ((rank < MOBA_TOPK) & (blk < own)) | (blk == own)
    bias_t = jnp.where(chosen, 0.0, NEG).astype(F32)
    pad = [jnp.zeros((rows, seq), F32) for rows in (lane_offset, LANES - nb - lane_offset)]
    parts = [part for part in (pad[0], bias_t, pad[1]) if part.shape[0]]
    return jnp.concatenate(parts, axis=0).T.astype(BF16)


def _moba_attn_kernel(q_ref, k_ref, v_ref, o_ref, *, seq, heads):
    nb = seq // MOBA_BLOCK
    half = A_HEAD_DIM // 2
    lane = lax.broadcasted_iota(jnp.int32, (seq, LANES), 1)
    key_block = lax.broadcasted_iota(jnp.int32, (seq, LANES), 0) // MOBA_BLOCK
    first = lane < half
    ones = jnp.ones((seq, LANES), BF16)
    qi_idx = lax.broadcasted_iota(jnp.int32, (MOBA_BLOCK, MOBA_BLOCK), 0)
    ki_idx = lax.broadcasted_iota(jnp.int32, (MOBA_BLOCK, MOBA_BLOCK), 1)
    causal = ki_idx <= qi_idx

    q, k_aug, v_aug, q_aug, offsets = [], [], [], [], []
    for h in range(heads):
        pair, second = divmod(h, 2)
        lo = pair * 2 * A_HEAD_DIM
        own = ~first if second else first
        offset = 0 if second else half
        bias_lanes = (lane >= offset) & (lane < offset + nb)
        onehot = (lane == key_block + offset).astype(BF16)
        zero = jnp.zeros((seq, LANES), BF16)
        q1 = jnp.where(own, q_ref[:, lo:lo + LANES], zero)
        q2 = jnp.where(own, q_ref[:, lo + LANES:lo + 2 * LANES], zero)
        q.append((q1, q2))
        k_aug.append(jnp.concatenate([jnp.where(bias_lanes, onehot, k_ref[:, lo:lo + LANES]),
                                      k_ref[:, lo + LANES:lo + 2 * LANES]], axis=1))
        v_aug.append(jnp.concatenate([v_ref[:, h * A_HEAD_DIM:(h + 1) * A_HEAD_DIM], ones], axis=1))
        offsets.append(offset)

    def scores(i, h):
        rows = slice(i * MOBA_BLOCK, (i + 1) * MOBA_BLOCK)
        if i == 0:
            qa = jnp.concatenate([q[h][0][rows, :], q[h][1][rows, :]], axis=1)
        else:
            qa = q_aug[h][rows, :]
        return _dot_nt(qa, k_aug[h][:(i + 1) * MOBA_BLOCK, :])

    def finish(i, h, s):
        s_diag = jnp.where(causal, s[:, i * MOBA_BLOCK:], NEG)
        s = s_diag if i == 0 else jnp.concatenate([s[:, :i * MOBA_BLOCK], s_diag], axis=1)
        m = jnp.max(s, axis=-1, keepdims=True)
        p = jnp.exp2(s - m).astype(BF16)
        o = _dot(p, v_aug[h][:(i + 1) * MOBA_BLOCK, :])
        o_ref[i * MOBA_BLOCK:(i + 1) * MOBA_BLOCK, h * A_HEAD_DIM:(h + 1) * A_HEAD_DIM] = (
            o[:, :A_HEAD_DIM] / o[:, A_HEAD_DIM:A_HEAD_DIM + 1]).astype(BF16)

    tasks = [(i, h) for i in range(nb) for h in range(heads)]
    pending = [scores(0, h) for h in range(heads)]
    for pair in range(heads // 2):
        cols = slice(pair * 2 * A_HEAD_DIM, (pair + 1) * 2 * A_HEAD_DIM)
        gates = _pair_gates(q_ref[:, cols], k_ref[:, cols], seq)
        for h in (2 * pair, 2 * pair + 1):
            bias = _moba_bias(gates[h % 2], seq, offsets[h])
            q_aug.append(jnp.concatenate([q[h][0] + bias, q[h][1]], axis=1))
    for t, (i, h) in enumerate(tasks):
        if t + heads < len(tasks):
            pending.append(scores(*tasks[t + heads]))
        finish(i, h, pending[t])


MOBA_HEADS_PER_STEP = 4


def _moba_attn(q, k, v, batch, seq):
    hps = MOBA_HEADS_PER_STEP
    spec = pl.BlockSpec((seq, hps * A_HEAD_DIM), lambda b, h: (b, h))
    return pl.pallas_call(
        functools.partial(_moba_attn_kernel, seq=seq, heads=hps),
        grid=(batch, A_HEADS // hps),
        in_specs=[spec, spec, spec],
        out_specs=spec,
        out_shape=jax.ShapeDtypeStruct(q.shape, BF16),
        compiler_params=pltpu.CompilerParams(dimension_semantics=("arbitrary", "arbitrary"),
                                             vmem_limit_bytes=VMEM_LIMIT),
        name="moba_attn",
    )(q, k, v)


WEIGHT_CAST_STEPS = 8


def _cast_slab(src_ref, dst_ref, step):
    rows = src_ref.shape[0]
    dst_ref[pl.ds(pl.multiple_of(step * rows, rows), rows), :] = src_ref[...].astype(BF16)


def _slab_spec(stacked, layer):
    _, rows, cols = stacked.shape
    return pl.BlockSpec((None, rows // WEIGHT_CAST_STEPS, cols),
                        lambda s: (layer, jnp.minimum(s, WEIGHT_CAST_STEPS - 1), 0))


def _layer_row_spec(cols, layer):
    return pl.BlockSpec((None, 1, cols), lambda *_: (layer, 0, 0), pipeline_mode=pl.Buffered(1))


def _row_spec(cols):
    return pl.BlockSpec((ROW_TILE, cols), lambda s: (jnp.maximum(s - WEIGHT_CAST_STEPS, 0), 0))


def _mix_mlp_kernel(*refs, gated):
    if gated:
        a_ref, gate_ref, wo_f_ref, x_ref, g_ref, w1_f_ref, w2_f_ref, o_ref, wo_ref, w1_ref, w2_ref, u_ref = refs
    else:
        a_ref, wo_f_ref, x_ref, g_ref, w1_f_ref, w2_f_ref, o_ref, wo_ref, w1_ref, w2_ref, u_ref = refs
    step = pl.program_id(0)

    @pl.when(step < WEIGHT_CAST_STEPS)
    def _cast():
        _cast_slab(wo_f_ref, wo_ref, step)
        _cast_slab(w1_f_ref, w1_ref, step)
        _cast_slab(w2_f_ref, w2_ref, step)

    @pl.when(step >= WEIGHT_CAST_STEPS)
    def _compute():
        a = a_ref[...]
        if gated:
            a = (a.astype(F32) * gate_ref[...].astype(F32)).astype(BF16)
        x1 = x_ref[...] + _dot(a, wo_ref[...])
        h = _rms(x1, g_ref[...]).astype(BF16)
        for c in range(D_FF // D_MODEL):
            cols = slice(c * D_MODEL, (c + 1) * D_MODEL)
            u = jnp.maximum(_dot(h, w1_ref[:, cols]), 0.0)
            u_ref[:, cols] = (u * u).astype(BF16)
        o_ref[...] = x1 + _dot(u_ref[...], w2_ref[...])


def _mix_mlp(a, gate, wo, wo_layer, x2, g, w1, w2, layer):
    t, kdim = a.shape
    gated = gate is not None
    acts = [a, gate] if gated else [a]
    return pl.pallas_call(
        functools.partial(_mix_mlp_kernel, gated=gated),
        grid=(WEIGHT_CAST_STEPS + t // ROW_TILE,),
        in_specs=[_row_spec(kdim)] * len(acts) + [
            _slab_spec(wo, wo_layer), _row_spec(D_MODEL), _layer_row_spec(D_MODEL, layer),
            _slab_spec(w1, layer), _slab_spec(w2, layer)],
        out_specs=_row_spec(D_MODEL),
        out_shape=jax.ShapeDtypeStruct(x2.shape, F32),
        scratch_shapes=[pltpu.VMEM(wo.shape[1:], BF16), pltpu.VMEM(w1.shape[1:], BF16),
                        pltpu.VMEM(w2.shape[1:], BF16), pltpu.VMEM((ROW_TILE, D_FF), BF16)],
        compiler_params=_params(),
        name="mix_mlp",
    )(*acts, wo, x2, g, w1, w2)


RET_PROJ_CHUNK = 512
RET_CHUNK = 256


def _ret_proj_kernel(x_ref, g_ref, w_f_ref, cos_ref, sin_ref, lg_ref, q_ref, k_ref, v_ref, sg_ref, w_ref):
    step = pl.program_id(0)
    half = R_QK_DIM // 2
    chunk = RET_PROJ_CHUNK
    n_chunks = 6 * D_MODEL // chunk

    @pl.when(step < WEIGHT_CAST_STEPS)
    def _cast():
        _cast_slab(w_f_ref, w_ref, step)

    def rope_store(y, lo, out_ref, sign, scale):
        local = (lax.broadcasted_iota(jnp.int32, (ROW_TILE, 1), 0) % RET_CHUNK).astype(F32)
        for j in range(chunk // R_QK_DIM):
            hh = lo // R_QK_DIM + j
            decay = jnp.exp(lg_ref[hh] * (sign * local)) * scale
            c, s = cos_ref[...] * decay, sin_ref[...] * decay
            x1 = y[:, j * R_QK_DIM:j * R_QK_DIM + half]
            x2 = y[:, j * R_QK_DIM + half:(j + 1) * R_QK_DIM]
            out_ref[:, hh * R_QK_DIM:hh * R_QK_DIM + half] = (x1 * c - x2 * s).astype(BF16)
            out_ref[:, hh * R_QK_DIM + half:(hh + 1) * R_QK_DIM] = (x2 * c + x1 * s).astype(BF16)

    def epilogue(c, y):
        col = c * chunk
        if col < D_MODEL:
            rope_store(y, col, q_ref, 1.0, 1.0)
        elif col < 2 * D_MODEL:
            rope_store(y, col - D_MODEL, k_ref, -1.0, R_QK_DIM ** -0.5)
        elif col < 4 * D_MODEL:
            v_ref[:, col - 2 * D_MODEL:col - 2 * D_MODEL + chunk] = y.astype(BF16)
        else:
            sg_ref[:, col - 4 * D_MODEL:col - 4 * D_MODEL + chunk] = (y * jax.nn.sigmoid(y)).astype(BF16)

    @pl.when(step >= WEIGHT_CAST_STEPS)
    def _compute():
        h = _rms(x_ref[...], g_ref[...]).astype(BF16)

        def project(c):
            return _dot(h, w_ref[:, c * chunk:(c + 1) * chunk])

        pending = project(0)
        for c in range(n_chunks):
            following = project(c + 1) if c + 1 < n_chunks else None
            epilogue(c, pending)
            pending = following


def _position_major(batch, seq):
    tiles_per_seq = seq // ROW_TILE

    def split(s):
        p, b = jnp.divmod(jnp.maximum(s - WEIGHT_CAST_STEPS, 0), batch)
        return p, b * tiles_per_seq + p

    return split


def _ret_proj(x2, g, g_layer, w, w_layer, cos, sin, log_g, batch, seq):
    t = x2.shape[0]
    half = R_QK_DIM // 2
    split = _position_major(batch, seq)
    rows = lambda cols: pl.BlockSpec((ROW_TILE, cols), lambda s: (split(s)[1], 0))
    table = pl.BlockSpec((ROW_TILE, half), lambda s: (split(s)[0], 0))
    return pl.pallas_call(
        _ret_proj_kernel,
        grid=(WEIGHT_CAST_STEPS + t // ROW_TILE,),
        in_specs=[rows(D_MODEL), _layer_row_spec(D_MODEL, g_layer), _slab_spec(w, w_layer),
                  table, table, _resident((R_HEADS, 1, 1))],
        out_specs=[rows(D_MODEL), rows(D_MODEL), rows(2 * D_MODEL), rows(2 * D_MODEL)],
        out_shape=[jax.ShapeDtypeStruct((t, D_MODEL), BF16), jax.ShapeDtypeStruct((t, D_MODEL), BF16),
                   jax.ShapeDtypeStruct((t, 2 * D_MODEL), BF16), jax.ShapeDtypeStruct((t, 2 * D_MODEL), BF16)],
        scratch_shapes=[pltpu.VMEM(w.shape[1:], BF16)],
        compiler_params=_params(),
        name="ret_proj",
    )(x2, g, w, cos, sin, log_g)


RET_HEADS_PER_STEP = 2


def _retention_kernel(q_ref, k_ref, v_ref, gch_ref, o_ref, *, seq, heads):
    c = RET_CHUNK
    n_chunks = seq // c
    ti = lax.broadcasted_iota(jnp.int32, (c, c), 0)
    ui = lax.broadcasted_iota(jnp.int32, (c, c), 1)
    causal = ui <= ti

    def chunk_matmuls(hh, n):
        rows = slice(n * c, (n + 1) * c)
        q = q_ref[rows, hh * R_QK_DIM:(hh + 1) * R_QK_DIM]
        k = k_ref[rows, hh * R_QK_DIM:(hh + 1) * R_QK_DIM]
        v = v_ref[rows, hh * R_V_DIM:(hh + 1) * R_V_DIM]
        return _dot_nt(q, k), _dot_tn(k, v)

    tasks = [(hh, n) for hh in range(heads) for n in range(n_chunks)]
    pending = chunk_matmuls(*tasks[0])
    decayed = None
    for t, (hh, n) in enumerate(tasks):
        rows = slice(n * c, (n + 1) * c)
        vcols = slice(hh * R_V_DIM, (hh + 1) * R_V_DIM)
        following = chunk_matmuls(*tasks[t + 1]) if t + 1 < len(tasks) else None
        s, update = pending
        y = _dot(jnp.where(causal, s, 0.0).astype(BF16), v_ref[rows, vcols])
        if n > 0:
            y = y + _dot(q_ref[rows, hh * R_QK_DIM:(hh + 1) * R_QK_DIM], decayed.astype(BF16))
        if n + 1 < n_chunks:
            decayed = (update if n == 0 else decayed + update) * gch_ref[hh]
        o_ref[rows, vcols] = (y * lax.rsqrt(jnp.mean(y * y, axis=-1, keepdims=True) + EPS)).astype(BF16)
        pending = following


def _retention(q, k, v, gch, batch, seq):
    hps = RET_HEADS_PER_STEP
    qk_spec = pl.BlockSpec((seq, hps * R_QK_DIM), lambda b, h: (b, h))
    v_spec = pl.BlockSpec((seq, hps * R_V_DIM), lambda b, h: (b, h))
    return pl.pallas_call(
        functools.partial(_retention_kernel, seq=seq, heads=hps),
        grid=(batch, R_HEADS // hps),
        in_specs=[qk_spec, qk_spec, v_spec, pl.BlockSpec((hps, 1, 1), lambda b, h: (h, 0, 0))],
        out_specs=v_spec,
        out_shape=jax.ShapeDtypeStruct(v.shape, BF16),
        compiler_params=pltpu.CompilerParams(dimension_semantics=("arbitrary", "arbitrary"),
                                             vmem_limit_bytes=VMEM_LIMIT),
        name="retention",
    )(q, k, v, gch)


def _rope_tables(seq, half):
    inv = ROPE_THETA ** (-jnp.arange(half, dtype=F32) / half)
    ang = jnp.arange(seq, dtype=jnp.int32).astype(F32)[:, None] * inv[None, :]
    return jnp.cos(ang), jnp.sin(ang)


def _retention_decay():
    log_g = jnp.log1p(-(2.0 ** (-5.0 - jnp.arange(R_HEADS, dtype=F32))))
    return log_g[:, None, None], jnp.exp(log_g * RET_CHUNK)[:, None, None]


def kernel(x, norm_mix_g, norm_mlp_g, a_w_qkv, a_q_gain, a_k_gain, a_w_o, r_w_in, r_w_out, mlp_w1, mlp_w2):
    batch, seq, d = x.shape
    assert d == D_MODEL and seq % ROW_TILE == 0 and seq % MOBA_BLOCK == 0 and ROW_TILE % RET_CHUNK == 0
    depth = norm_mix_g.shape[0]
    x2 = x.reshape(batch * seq, d)
    norm_mix_g = norm_mix_g.reshape(depth, 1, d)
    norm_mlp_g = norm_mlp_g.reshape(depth, 1, d)

    cos_a, sin_a = _rope_tables(seq, A_HEAD_DIM // 2)
    cos_a = jnp.concatenate([cos_a, cos_a], axis=1)
    sin_a = jnp.concatenate([sin_a, sin_a], axis=1)
    cos_r, sin_r = _rope_tables(seq, R_QK_DIM // 2)
    log_g, gch = _retention_decay()

    a_q_gain = a_q_gain.reshape(-1, 1, A_HEAD_DIM)
    a_k_gain = a_k_gain.reshape(-1, 1, A_HEAD_DIM)

    for i in range(depth):
        j = i // 2
        if i % 2 == 0:
            q, k, v = _moba_qkv(x2, norm_mix_g, i, a_w_qkv, j, a_q_gain, a_k_gain, cos_a, sin_a, batch, seq)
            a, gate, w_o = _moba_attn(q, k, v, batch, seq), None, a_w_o
        else:
            q, k, v, gate = _ret_proj(x2, norm_mix_g, i, r_w_in, j, cos_r, sin_r, log_g, batch, seq)
            a, w_o = _retention(q, k, v, gch, batch, seq), r_w_out
        x2 = _mix_mlp(a, gate, w_o, j, x2, norm_mlp_g, mlp_w1, mlp_w2, i)
    return x2.reshape(batch, seq, d)
```

```python
import functools

import jax
import jax.numpy as jnp
import numpy as np
from jax import lax
from jax.experimental import pallas as pl
from jax.experimental.pallas import tpu as pltpu

D_MODEL = 1024
A_HEADS = 8
A_HEAD_DIM = D_MODEL // A_HEADS
MOBA_BLOCK = 256
MOBA_TOPK = 3
R_HEADS = 4
R_QK_DIM = D_MODEL // R_HEADS
R_V_DIM = 2 * D_MODEL // R_HEADS
D_FF = 4 * D_MODEL
ROPE_THETA = 10000.0
EPS = 1e-6
NEG = -1e30

LANES = 128
ROW_TILE = 512
QKV_ROW_TILE = 1024
VMEM_LIMIT = 56 * 1024 * 1024

BF16 = jnp.bfloat16
F32 = jnp.float32


def _dot(a, b):
    return jnp.dot(a, b, preferred_element_type=F32)


def _dot_nt(a, b):
    return lax.dot_general(a, b, (((1,), (1,)), ((), ())), preferred_element_type=F32)


def _dot_tn(a, b):
    return lax.dot_general(a, b, (((0,), (0,)), ((), ())), preferred_element_type=F32)


def _rms(x, g):
    return x * lax.rsqrt(jnp.mean(x * x, axis=-1, keepdims=True) + EPS) * g


def _params():
    return pltpu.CompilerParams(dimension_semantics=("arbitrary",), vmem_limit_bytes=VMEM_LIMIT)


def _resident(shape):
    nd = len(shape)
    return pl.BlockSpec(shape, lambda *_: (0,) * nd, pipeline_mode=pl.Buffered(1))


def _pair_halves(x, first):
    half = A_HEAD_DIM // 2
    c0, c1 = x[:, :LANES], x[:, LANES:]
    return (jnp.where(first, c0, pltpu.roll(c1, half, axis=1)),
            jnp.where(first, pltpu.roll(c0, half, axis=1), c1))


def _moba_qkv_kernel(x_ref, g_ref, w_f_ref, qg_ref, kg_ref, cos_ref, sin_ref, q_ref, k_ref, v_ref, w_ref):
    step = pl.program_id(0)
    chunk = 2 * A_HEAD_DIM
    n_chunks = 3 * D_MODEL // chunk

    @pl.when(step < WEIGHT_CAST_STEPS)
    def _cast():
        slab = w_f_ref.shape[0]
        rows = pl.ds(pl.multiple_of(step * slab, slab), slab)
        first = lax.broadcasted_iota(jnp.int32, (slab, LANES), 1) < A_HEAD_DIM // 2
        for c in range(n_chunks):
            cols = slice(c * chunk, (c + 1) * chunk)
            if c * chunk < 2 * D_MODEL:
                a, b = _pair_halves(w_f_ref[:, cols], first)
                w_ref[rows, c * chunk:c * chunk + LANES] = a.astype(BF16)
                w_ref[rows, c * chunk + LANES:(c + 1) * chunk] = b.astype(BF16)
            else:
                w_ref[rows, cols] = w_f_ref[:, cols].astype(BF16)

    @pl.when(step >= WEIGHT_CAST_STEPS)
    def _compute():
        h = _rms(x_ref[...], g_ref[...]).astype(BF16)
        cos = cos_ref[...]
        sin = sin_ref[...]
        first = lax.broadcasted_iota(jnp.int32, (1, LANES), 1) < A_HEAD_DIM // 2

        def rope_tables(gain_ref, scale):
            gain = jnp.broadcast_to(gain_ref[...], (8, A_HEAD_DIM)) * scale
            turned = pltpu.roll(gain, A_HEAD_DIM // 2, axis=1)
            g1 = jnp.where(first, gain, turned)[0:1]
            g2 = jnp.where(first, turned, gain)[0:1]
            return cos * g1, sin * g2, cos * g2, sin * g1

        tables = (rope_tables(qg_ref, A_HEAD_DIM ** -0.5 * float(np.log2(np.e))), rope_tables(kg_ref, 1.0))
        outs = (q_ref, k_ref, v_ref)

        def project(c):
            return _dot(h, w_ref[:, c * chunk:(c + 1) * chunk])

        def head_rsqrt(t, lanes):
            return lax.rsqrt(jnp.sum(jnp.where(lanes, t, 0.0), axis=-1, keepdims=True) * (1.0 / A_HEAD_DIM) + EPS)

        def epilogue(c, y):
            section, lo = divmod(c * chunk, D_MODEL)
            if section == 2:
                v_ref[:, lo:lo + chunk] = y.astype(BF16)
                return
            c1, s2, c2, s1 = tables[section]
            y1 = y[:, :LANES]
            y2 = y[:, LANES:]
            t = y1 * y1 + y2 * y2
            r = jnp.where(first, head_rsqrt(t, first), head_rsqrt(t, ~first))
            outs[section][:, lo:lo + LANES] = ((y1 * c1 - y2 * s2) * r).astype(BF16)
            outs[section][:, lo + LANES:lo + chunk] = ((y2 * c2 + y1 * s1) * r).astype(BF16)

        pending = project(0)
        for c in range(n_chunks):
            following = project(c + 1) if c + 1 < n_chunks else None
            epilogue(c, pending)
            pending = following


def _moba_qkv(x2, g, g_layer, w, w_layer, qg, kg, cos, sin, batch, seq):
    t = x2.shape[0]
    tile = QKV_ROW_TILE
    split = _position_major(batch, seq, tile)
    rows = pl.BlockSpec((tile, D_MODEL), lambda s: (split(s)[1], 0))
    table = pl.BlockSpec((tile, A_HEAD_DIM), lambda s: (split(s)[0], 0))
    out = jax.ShapeDtypeStruct((t, D_MODEL), BF16)
    return pl.pallas_call(
        _moba_qkv_kernel,
        grid=(WEIGHT_CAST_STEPS + t // tile,),
        in_specs=[rows, _layer_row_spec(D_MODEL, g_layer), _slab_spec(w, w_layer),
                  _layer_row_spec(A_HEAD_DIM, w_layer), _layer_row_spec(A_HEAD_DIM, w_layer), table, table],
        out_specs=[rows] * 3,
        out_shape=[out, out, out],
        scratch_shapes=[pltpu.VMEM(w.shape[1:], BF16)],
        compiler_params=_params(),
        name="moba_qkv",
    )(x2, g, w, qg, kg, cos, sin)


def _pair_gates(q_pair, k_pair, seq):
    nb = seq // MOBA_BLOCK
    kmean = jnp.mean(k_pair.astype(F32).reshape(nb, MOBA_BLOCK, 2 * LANES), axis=1)
    first = lax.broadcasted_iota(jnp.int32, (nb, 2 * LANES), 1) % LANES < A_HEAD_DIM // 2
    terms = []
    for own in (first, ~first):
        km = jnp.where(own, kmean, 0.0)
        hi = km.astype(BF16)
        terms += [hi, (km - hi.astype(F32)).astype(BF16)]
    g = _dot_nt(jnp.concatenate(terms, axis=0), q_pair)
    return g[0:nb] + g[nb:2 * nb], g[2 * nb:3 * nb] + g[3 * nb:4 * nb]


def _moba_bias(gate, seq, lane_offset):
    nb = seq // MOBA_BLOCK
    blk = lax.broadcasted_iota(jnp.int32, (nb, seq), 0)
    own = lax.broadcasted_iota(jnp.int32, (nb, seq), 1) // MOBA_BLOCK
    gate = jnp.where(blk < own, gate, NEG)
    rank = jnp.zeros((nb, seq), jnp.int32)
    for j in range(nb):
        gj = gate[j:j + 1, :]
        beats = (gj > gate) | ((gj == gate) & (blk > j))
        rank = rank + beats.astype(jnp.int32)
    chosen = ((rank < MOBA_TOPK) & (blk < own)) | (blk == own)
    bias_t = jnp.where(chosen, 0.0, NEG).astype(F32)
    pad = [jnp.zeros((rows, seq), F32) for rows in (lane_offset, LANES - nb - lane_offset)]
    parts = [part for part in (pad[0], bias_t, pad[1]) if part.shape[0]]
    return jnp.concatenate(parts, axis=0).T.astype(BF16)


def _moba_attn_kernel(q_ref, k_ref, v_ref, o_ref, *, seq, heads):
    nb = seq // MOBA_BLOCK
    half = A_HEAD_DIM // 2
    lane = lax.broadcasted_iota(jnp.int32, (seq, LANES), 1)
    key_block = lax.broadcasted_iota(jnp.int32, (seq, LANES), 0) // MOBA_BLOCK
    first = lane < half
    ones = jnp.ones((seq, LANES), BF16)
    qi_idx = lax.broadcasted_iota(jnp.int32, (MOBA_BLOCK, MOBA_BLOCK), 0)
    ki_idx = lax.broadcasted_iota(jnp.int32, (MOBA_BLOCK, MOBA_BLOCK), 1)
    causal = ki_idx <= qi_idx

    q, k_aug, v_aug, q_aug, offsets = [], [], [], [], []
    for h in range(heads):
        pair, second = divmod(h, 2)
        lo = pair * 2 * A_HEAD_DIM
        own = ~first if second else first
        offset = 0 if second else half
        bias_lanes = (lane >= offset) & (lane < offset + nb)
        onehot = (lane == key_block + offset).astype(BF16)
        zero = jnp.zeros((seq, LANES), BF16)
        q1 = jnp.where(own, q_ref[:, lo:lo + LANES], zero)
        q2 = jnp.where(own, q_ref[:, lo + LANES:lo + 2 * LANES], zero)
        q.append((q1, q2))
        k_aug.append(jnp.concatenate([jnp.where(bias_lanes, onehot, k_ref[:, lo:lo + LANES]),
                                      k_ref[:, lo + LANES:lo + 2 * LANES]], axis=1))
        v_aug.append(jnp.concatenate([v_ref[:, h * A_HEAD_DIM:(h + 1) * A_HEAD_DIM], ones], axis=1))
        offsets.append(offset)

    def scores(i, h):
        rows = slice(i * MOBA_BLOCK, (i + 1) * MOBA_BLOCK)
        if i == 0:
            qa = jnp.concatenate([q[h][0][rows, :], q[h][1][rows, :]], axis=1)
        else:
            qa = q_aug[h][rows, :]
        return _dot_nt(qa, k_aug[h][:(i + 1) * MOBA_BLOCK, :])

    def finish(i, h, s):
        s_diag = jnp.where(causal, s[:, i * MOBA_BLOCK:], NEG)
        s = s_diag if i == 0 else jnp.concatenate([s[:, :i * MOBA_BLOCK], s_diag], axis=1)
        m = jnp.max(s, axis=-1, keepdims=True)
        p = jnp.exp2(s - m).astype(BF16)
        o = _dot(p, v_aug[h][:(i + 1) * MOBA_BLOCK, :])
        o_ref[i * MOBA_BLOCK:(i + 1) * MOBA_BLOCK, h * A_HEAD_DIM:(h + 1) * A_HEAD_DIM] = (
            o[:, :A_HEAD_DIM] / o[:, A_HEAD_DIM:A_HEAD_DIM + 1]).astype(BF16)

    tasks = [(i, h) for i in range(nb) for h in range(heads)]
    pending = [scores(0, h) for h in range(heads)]
    for pair in range(heads // 2):
        cols = slice(pair * 2 * A_HEAD_DIM, (pair + 1) * 2 * A_HEAD_DIM)
        gates = _pair_gates(q_ref[:, cols], k_ref[:, cols], seq)
        for h in (2 * pair, 2 * pair + 1):
            bias = _moba_bias(gates[h % 2], seq, offsets[h])
            q_aug.append(jnp.concatenate([q[h][0] + bias, q[h][1]], axis=1))
    for t, (i, h) in enumerate(tasks):
        if t + heads < len(tasks):
            pending.append(scores(*tasks[t + heads]))
        finish(i, h, pending[t])


MOBA_HEADS_PER_STEP = 4


def _moba_attn(q, k, v, batch, seq):
    hps = MOBA_HEADS_PER_STEP
    spec = pl.BlockSpec((seq, hps * A_HEAD_DIM), lambda b, h: (b, h))
    return pl.pallas_call(
        functools.partial(_moba_attn_kernel, seq=seq, heads=hps),
        grid=(batch, A_HEADS // hps),
        in_specs=[spec, spec, spec],
        out_specs=spec,
        out_shape=jax.ShapeDtypeStruct(q.shape, BF16),
        compiler_params=pltpu.CompilerParams(dimension_semantics=("arbitrary", "arbitrary"),
                                             vmem_limit_bytes=VMEM_LIMIT),
        name="moba_attn",
    )(q, k, v)


WEIGHT_CAST_STEPS = 8


def _cast_slab(src_ref, dst_ref, step):
    rows = src_ref.shape[0]
    dst_ref[pl.ds(pl.multiple_of(step * rows, rows), rows), :] = src_ref[...].astype(BF16)


def _slab_spec(stacked, layer):
    _, rows, cols = stacked.shape
    return pl.BlockSpec((None, rows // WEIGHT_CAST_STEPS, cols),
                        lambda s: (layer, jnp.minimum(s, WEIGHT_CAST_STEPS - 1), 0))


def _layer_row_spec(cols, layer):
    return pl.BlockSpec((None, 1, cols), lambda *_: (layer, 0, 0), pipeline_mode=pl.Buffered(1))


def _row_spec(cols):
    return pl.BlockSpec((ROW_TILE, cols), lambda s: (jnp.maximum(s - WEIGHT_CAST_STEPS, 0), 0))


def _mix_mlp_kernel(*refs, gated):
    if gated:
        a_ref, gate_ref, wo_f_ref, x_ref, g_ref, w1_f_ref, w2_f_ref, o_ref, wo_ref, w1_ref, w2_ref, u_ref = refs
    else:
        a_ref, wo_f_ref, x_ref, g_ref, w1_f_ref, w2_f_ref, o_ref, wo_ref, w1_ref, w2_ref, u_ref = refs
    step = pl.program_id(0)

    @pl.when(step < WEIGHT_CAST_STEPS)
    def _cast():
        _cast_slab(wo_f_ref, wo_ref, step)
        _cast_slab(w1_f_ref, w1_ref, step)
        _cast_slab(w2_f_ref, w2_ref, step)

    @pl.when(step >= WEIGHT_CAST_STEPS)
    def _compute():
        a = a_ref[...]
        if gated:
            a = (a.astype(F32) * gate_ref[...].astype(F32)).astype(BF16)
        x1 = x_ref[...] + _dot(a, wo_ref[...])
        h = _rms(x1, g_ref[...]).astype(BF16)
        for c in range(D_FF // D_MODEL):
            cols = slice(c * D_MODEL, (c + 1) * D_MODEL)
            u = jnp.maximum(_dot(h, w1_ref[:, cols]), 0.0)
            u_ref[:, cols] = (u * u).astype(BF16)
        o_ref[...] = x1 + _dot(u_ref[...], w2_ref[...])


def _mix_mlp(a, gate, wo, wo_layer, x2, g, w1, w2, layer):
    t, kdim = a.shape
    gated = gate is not None
    acts = [a, gate] if gated else [a]
    return pl.pallas_call(
        functools.partial(_mix_mlp_kernel, gated=gated),
        grid=(WEIGHT_CAST_STEPS + t // ROW_TILE,),
        in_specs=[_row_spec(kdim)] * len(acts) + [
            _slab_spec(wo, wo_layer), _row_spec(D_MODEL), _layer_row_spec(D_MODEL, layer),
            _slab_spec(w1, layer), _slab_spec(w2, layer)],
        out_specs=_row_spec(D_MODEL),
        out_shape=jax.ShapeDtypeStruct(x2.shape, F32),
        scratch_shapes=[pltpu.VMEM(wo.shape[1:], BF16), pltpu.VMEM(w1.shape[1:], BF16),
                        pltpu.VMEM(w2.shape[1:], BF16), pltpu.VMEM((ROW_TILE, D_FF), BF16)],
        compiler_params=_params(),
        name="mix_mlp",
    )(*acts, wo, x2, g, w1, w2)


RET_PROJ_CHUNK = 512
RET_CHUNK = 256


def _ret_proj_kernel(x_ref, g_ref, w_f_ref, cos_ref, sin_ref, lg_ref, q_ref, k_ref, v_ref, sg_ref, w_ref):
    step = pl.program_id(0)
    half = R_QK_DIM // 2
    chunk = RET_PROJ_CHUNK
    n_chunks = 6 * D_MODEL // chunk

    @pl.when(step < WEIGHT_CAST_STEPS)
    def _cast():
        _cast_slab(w_f_ref, w_ref, step)

    def rope_store(y, lo, out_ref, sign, scale):
        local = (lax.broadcasted_iota(jnp.int32, (ROW_TILE, 1), 0) % RET_CHUNK).astype(F32)
        for j in range(chunk // R_QK_DIM):
            hh = lo // R_QK_DIM + j
            decay = jnp.exp(lg_ref[hh] * (sign * local)) * scale
            c, s = cos_ref[...] * decay, sin_ref[...] * decay
            x1 = y[:, j * R_QK_DIM:j * R_QK_DIM + half]
            x2 = y[:, j * R_QK_DIM + half:(j + 1) * R_QK_DIM]
            out_ref[:, hh * R_QK_DIM:hh * R_QK_DIM + half] = (x1 * c - x2 * s).astype(BF16)
            out_ref[:, hh * R_QK_DIM + half:(hh + 1) * R_QK_DIM] = (x2 * c + x1 * s).astype(BF16)

    def epilogue(c, y):
        col = c * chunk
        if col < D_MODEL:
            rope_store(y, col, q_ref, 1.0, 1.0)
        elif col < 2 * D_MODEL:
            rope_store(y, col - D_MODEL, k_ref, -1.0, R_QK_DIM ** -0.5)
        elif col < 4 * D_MODEL:
            v_ref[:, col - 2 * D_MODEL:col - 2 * D_MODEL + chunk] = y.astype(BF16)
        else:
            sg_ref[:, col - 4 * D_MODEL:col - 4 * D_MODEL + chunk] = (y * jax.nn.sigmoid(y)).astype(BF16)

    @pl.when(step >= WEIGHT_CAST_STEPS)
    def _compute():
        h = _rms(x_ref[...], g_ref[...]).astype(BF16)

        def project(c):
            return _dot(h, w_ref[:, c * chunk:(c + 1) * chunk])

        pending = project(0)
        for c in range(n_chunks):
            following = project(c + 1) if c + 1 < n_chunks else None
            epilogue(c, pending)
            pending = following


def _position_major(batch, seq, tile):
    tiles_per_seq = seq // tile

    def split(s):
        p, b = jnp.divmod(jnp.maximum(s - WEIGHT_CAST_STEPS, 0), batch)
        return p, b * tiles_per_seq + p

    return split


def _ret_proj(x2, g, g_layer, w, w_layer, cos, sin, log_g, batch, seq):
    t = x2.shape[0]
    half = R_QK_DIM // 2
    split = _position_major(batch, seq, ROW_TILE)
    rows = lambda cols: pl.BlockSpec((ROW_TILE, cols), lambda s: (split(s)[1], 0))
    table = pl.BlockSpec((ROW_TILE, half), lambda s: (split(s)[0], 0))
    return pl.pallas_call(
        _ret_proj_kernel,
        grid=(WEIGHT_CAST_STEPS + t // ROW_TILE,),
        in_specs=[rows(D_MODEL), _layer_row_spec(D_MODEL, g_layer), _slab_spec(w, w_layer),
                  table, table, _resident((R_HEADS, 1, 1))],
        out_specs=[rows(D_MODEL), rows(D_MODEL), rows(2 * D_MODEL), rows(2 * D_MODEL)],
        out_shape=[jax.ShapeDtypeStruct((t, D_MODEL), BF16), jax.ShapeDtypeStruct((t, D_MODEL), BF16),
                   jax.ShapeDtypeStruct((t, 2 * D_MODEL), BF16), jax.ShapeDtypeStruct((t, 2 * D_MODEL), BF16)],
        scratch_shapes=[pltpu.VMEM(w.shape[1:], BF16)],
        compiler_params=_params(),
        name="ret_proj",
    )(x2, g, w, cos, sin, log_g)


RET_HEADS_PER_STEP = 2


def _retention_kernel(q_ref, k_ref, v_ref, gch_ref, o_ref, *, seq, heads):
    c = RET_CHUNK
    n_chunks = seq // c
    ti = lax.broadcasted_iota(jnp.int32, (c, c), 0)
    ui = lax.broadcasted_iota(jnp.int32, (c, c), 1)
    causal = ui <= ti

    def chunk_matmuls(hh, n):
        rows = slice(n * c, (n + 1) * c)
        q = q_ref[rows, hh * R_QK_DIM:(hh + 1) * R_QK_DIM]
        k = k_ref[rows, hh * R_QK_DIM:(hh + 1) * R_QK_DIM]
        v = v_ref[rows, hh * R_V_DIM:(hh + 1) * R_V_DIM]
        return _dot_nt(q, k), _dot_tn(k, v)

    tasks = [(hh, n) for hh in range(heads) for n in range(n_chunks)]
    pending = chunk_matmuls(*tasks[0])
    decayed = None
    for t, (hh, n) in enumerate(tasks):
        rows = slice(n * c, (n + 1) * c)
        vcols = slice(hh * R_V_DIM, (hh + 1) * R_V_DIM)
        following = chunk_matmuls(*tasks[t + 1]) if t + 1 < len(tasks) else None
        s, update = pending
        y = _dot(jnp.where(causal, s, 0.0).astype(BF16), v_ref[rows, vcols])
        if n > 0:
            y = y + _dot(q_ref[rows, hh * R_QK_DIM:(hh + 1) * R_QK_DIM], decayed.astype(BF16))
        if n + 1 < n_chunks:
            decayed = (update if n == 0 else decayed + update) * gch_ref[hh]
        o_ref[rows, vcols] = (y * lax.rsqrt(jnp.mean(y * y, axis=-1, keepdims=True) + EPS)).astype(BF16)
        pending = following


def _retention(q, k, v, gch, batch, seq):
    hps = RET_HEADS_PER_STEP
    qk_spec = pl.BlockSpec((seq, hps * R_QK_DIM), lambda b, h: (b, h))
    v_spec = pl.BlockSpec((seq, hps * R_V_DIM), lambda b, h: (b, h))
    return pl.pallas_call(
        functools.partial(_retention_kernel, seq=seq, heads=hps),
        grid=(batch, R_HEADS // hps),
        in_specs=[qk_spec, qk_spec, v_spec, pl.BlockSpec((hps, 1, 1), lambda b, h: (h, 0, 0))],
        out_specs=v_spec,
        out_shape=jax.ShapeDtypeStruct(v.shape, BF16),
        compiler_params=pltpu.CompilerParams(dimension_semantics=("arbitrary", "arbitrary"),
                                             vmem_limit_bytes=VMEM_LIMIT),
        name="retention",
    )(q, k, v, gch)


def _rope_tables(seq, half):
    inv = ROPE_THETA ** (-jnp.arange(half, dtype=F32) / half)
    ang = jnp.arange(seq, dtype=jnp.int32).astype(F32)[:, None] * inv[None, :]
    return jnp.cos(ang), jnp.sin(ang)


def _retention_decay():
    log_g = jnp.log1p(-(2.0 ** (-5.0 - jnp.arange(R_HEADS, dtype=F32))))
    return log_g[:, None, None], jnp.exp(log_g * RET_CHUNK)[:, None, None]


def kernel(x, norm_mix_g, norm_mlp_g, a_w_qkv, a_q_gain, a_k_gain, a_w_o, r_w_in, r_w_out, mlp_w1, mlp_w2):
    batch, seq, d = x.shape
    assert d == D_MODEL and seq % ROW_TILE == 0 and seq % QKV_ROW_TILE == 0 and seq % MOBA_BLOCK == 0
    assert ROW_TILE % RET_CHUNK == 0
    depth = norm_mix_g.shape[0]
    x2 = x.reshape(batch * seq, d)
    norm_mix_g = norm_mix_g.reshape(depth, 1, d)
    norm_mlp_g = norm_mlp_g.reshape(depth, 1, d)

    cos_a, sin_a = _rope_tables(seq, A_HEAD_DIM // 2)
    cos_a = jnp.concatenate([cos_a, cos_a], axis=1)
    sin_a = jnp.concatenate([sin_a, sin_a], axis=1)
    cos_r, sin_r = _rope_tables(seq, R_QK_DIM // 2)
    log_g, gch = _retention_decay()

    a_q_gain = a_q_gain.reshape(-1, 1, A_HEAD_DIM)
    a_k_gain = a_k_gain.reshape(-1, 1, A_HEAD_DIM)

    for i in range(depth):
        j = i // 2
        if i % 2 == 0:
            q, k, v = _moba_qkv(x2, norm_mix_g, i, a_w_qkv, j, a_q_gain, a_k_gain, cos_a, sin_a, batch, seq)
            a, gate, w_o = _moba_attn(q, k, v, batch, seq), None, a_w_o
        else:
            q, k, v, gate = _ret_proj(x2, norm_mix_g, i, r_w_in, j, cos_r, sin_r, log_g, batch, seq)
            a, w_o = _retention(q, k, v, gch, batch, seq), r_w_out
        x2 = _mix_mlp(a, gate, w_o, j, x2, norm_mlp_g, mlp_w1, mlp_w2, i)
    return x2.reshape(batch, seq, d)
```

```python
import functools

import jax
import jax.numpy as jnp
import numpy as np
from jax import lax
from jax.experimental import pallas as pl
from jax.experimental.pallas import tpu as pltpu

D_MODEL = 1024
A_HEADS = 8
A_HEAD_DIM = D_MODEL // A_HEADS
MOBA_BLOCK = 256
MOBA_TOPK = 3
R_HEADS = 4
R_QK_DIM = D_MODEL // R_HEADS
R_V_DIM = 2 * D_MODEL // R_HEADS
D_FF = 4 * D_MODEL
ROPE_THETA = 10000.0
EPS = 1e-6
NEG = -1e30

LANES = 128
ROW_TILE = 512
QKV_ROW_TILE = 1024
VMEM_LIMIT = 56 * 1024 * 1024

BF16 = jnp.bfloat16
F32 = jnp.float32


def _dot(a, b):
    return jnp.dot(a, b, preferred_element_type=F32)


def _dot_nt(a, b):
    return lax.dot_general(a, b, (((1,), (1,)), ((), ())), preferred_element_type=F32)


def _dot_tn(a, b):
    return lax.dot_general(a, b, (((0,), (0,)), ((), ())), preferred_element_type=F32)


def _rms(x, g):
    return x * lax.rsqrt(jnp.mean(x * x, axis=-1, keepdims=True) + EPS) * g


def _rms_split(x, g):
    r = lax.rsqrt(jnp.mean(x * x, axis=-1, keepdims=True) + EPS)
    return (x * g).astype(BF16), r


def _params():
    return pltpu.CompilerParams(dimension_semantics=("arbitrary",), vmem_limit_bytes=VMEM_LIMIT)


def _resident(shape):
    nd = len(shape)
    return pl.BlockSpec(shape, lambda *_: (0,) * nd, pipeline_mode=pl.Buffered(1))


def _pair_halves(x, first):
    half = A_HEAD_DIM // 2
    c0, c1 = x[:, :LANES], x[:, LANES:]
    return (jnp.where(first, c0, pltpu.roll(c1, half, axis=1)),
            jnp.where(first, pltpu.roll(c0, half, axis=1), c1))


def _moba_qkv_kernel(x_ref, g_ref, w_f_ref, qg_ref, kg_ref, cos_ref, sin_ref, q_ref, k_ref, v_ref, w_ref):
    step = pl.program_id(0)
    chunk = 2 * A_HEAD_DIM
    n_chunks = 3 * D_MODEL // chunk

    @pl.when(step < WEIGHT_CAST_STEPS)
    def _cast():
        slab = w_f_ref.shape[0]
        rows = pl.ds(pl.multiple_of(step * slab, slab), slab)
        first = lax.broadcasted_iota(jnp.int32, (slab, LANES), 1) < A_HEAD_DIM // 2
        for c in range(n_chunks):
            cols = slice(c * chunk, (c + 1) * chunk)
            if c * chunk < 2 * D_MODEL:
                a, b = _pair_halves(w_f_ref[:, cols], first)
                w_ref[rows, c * chunk:c * chunk + LANES] = a.astype(BF16)
                w_ref[rows, c * chunk + LANES:(c + 1) * chunk] = b.astype(BF16)
            else:
                w_ref[rows, cols] = w_f_ref[:, cols].astype(BF16)

    @pl.when(step >= WEIGHT_CAST_STEPS)
    def _compute():
        h = _rms(x_ref[...], g_ref[...]).astype(BF16)
        cos = cos_ref[...]
        sin = sin_ref[...]
        first = lax.broadcasted_iota(jnp.int32, (1, LANES), 1) < A_HEAD_DIM // 2

        def rope_tables(gain_ref, scale):
            gain = jnp.broadcast_to(gain_ref[...], (8, A_HEAD_DIM)) * scale
            turned = pltpu.roll(gain, A_HEAD_DIM // 2, axis=1)
            g1 = jnp.where(first, gain, turned)[0:1]
            g2 = jnp.where(first, turned, gain)[0:1]
            return cos * g1, sin * g2, cos * g2, sin * g1

        tables = (rope_tables(qg_ref, A_HEAD_DIM ** -0.5 * float(np.log2(np.e))), rope_tables(kg_ref, 1.0))
        outs = (q_ref, k_ref, v_ref)

        def project(c):
            return _dot(h, w_ref[:, c * chunk:(c + 1) * chunk])

        def head_rsqrt(t, lanes):
            return lax.rsqrt(jnp.sum(jnp.where(lanes, t, 0.0), axis=-1, keepdims=True) * (1.0 / A_HEAD_DIM) + EPS)

        def epilogue(c, y):
            section, lo = divmod(c * chunk, D_MODEL)
            if section == 2:
                v_ref[:, lo:lo + chunk] = y.astype(BF16)
                return
            c1, s2, c2, s1 = tables[section]
            y1 = y[:, :LANES]
            y2 = y[:, LANES:]
            t = y1 * y1 + y2 * y2
            r = jnp.where(first, head_rsqrt(t, first), head_rsqrt(t, ~first))
            outs[section][:, lo:lo + LANES] = ((y1 * c1 - y2 * s2) * r).astype(BF16)
            outs[section][:, lo + LANES:lo + chunk] = ((y2 * c2 + y1 * s1) * r).astype(BF16)

        pending = project(0)
        for c in range(n_chunks):
            following = project(c + 1) if c + 1 < n_chunks else None
            epilogue(c, pending)
            pending = following


def _moba_qkv(x2, g, g_layer, w, w_layer, qg, kg, cos, sin, batch, seq):
    t = x2.shape[0]
    tile = QKV_ROW_TILE
    split = _position_major(batch, seq, tile)
    rows = pl.BlockSpec((tile, D_MODEL), lambda s: (split(s)[1], 0))
    table = pl.BlockSpec((tile, A_HEAD_DIM), lambda s: (split(s)[0], 0))
    out = jax.ShapeDtypeStruct((t, D_MODEL), BF16)
    return pl.pallas_call(
        _moba_qkv_kernel,
        grid=(WEIGHT_CAST_STEPS + t // tile,),
        in_specs=[rows, _layer_row_spec(D_MODEL, g_layer), _slab_spec(w, w_layer),
                  _layer_row_spec(A_HEAD_DIM, w_layer), _layer_row_spec(A_HEAD_DIM, w_layer), table, table],
        out_specs=[rows] * 3,
        out_shape=[out, out, out],
        scratch_shapes=[pltpu.VMEM(w.shape[1:], BF16)],
        compiler_params=_params(),
        name="moba_qkv",
    )(x2, g, w, qg, kg, cos, sin)


def _pair_gates(q_pair, k_pair, seq):
    nb = seq // MOBA_BLOCK
    kmean = jnp.mean(k_pair.astype(F32).reshape(nb, MOBA_BLOCK, 2 * LANES), axis=1)
    first = lax.broadcasted_iota(jnp.int32, (nb, 2 * LANES), 1) % LANES < A_HEAD_DIM // 2
    terms = []
    for own in (first, ~first):
        km = jnp.where(own, kmean, 0.0)
        hi = km.astype(BF16)
        terms += [hi, (km - hi.astype(F32)).astype(BF16)]
    g = _dot_nt(jnp.concatenate(terms, axis=0), q_pair)
    return g[0:nb] + g[nb:2 * nb], g[2 * nb:3 * nb] + g[3 * nb:4 * nb]


def _moba_bias(gate, seq, lane_offset):
    nb = seq // MOBA_BLOCK
    blk = lax.broadcasted_iota(jnp.int32, (nb, seq), 0)
    own = lax.broadcasted_iota(jnp.int32, (nb, seq), 1) // MOBA_BLOCK
    gate = jnp.where(blk < own, gate, NEG)
    rank = jnp.zeros((nb, seq), jnp.int32)
    for j in range(nb):
        gj = gate[j:j + 1, :]
        beats = (gj > gate) | ((gj == gate) & (blk > j))
        rank = rank + beats.astype(jnp.int32)
    chosen = ((rank < MOBA_TOPK) & (blk < own)) | (blk == own)
    bias_t = jnp.where(chosen, 0.0, NEG).astype(F32)
    pad = [jnp.zeros((rows, seq), F32) for rows in (lane_offset, LANES - nb - lane_offset)]
    parts = [part for part in (pad[0], bias_t, pad[1]) if part.shape[0]]
    return jnp.concatenate(parts, axis=0).T.astype(BF16)


def _moba_attn_kernel(q_ref, k_ref, v_ref, o_ref, *, seq, heads):
    nb = seq // MOBA_BLOCK
    half = A_HEAD_DIM // 2
    lane = lax.broadcasted_iota(jnp.int32, (seq, LANES), 1)
    key_block = lax.broadcasted_iota(jnp.int32, (seq, LANES), 0) // MOBA_BLOCK
    first = lane < half
    ones = jnp.ones((seq, LANES), BF16)
    qi_idx = lax.broadcasted_iota(jnp.int32, (MOBA_BLOCK, MOBA_BLOCK), 0)
    ki_idx = lax.broadcasted_iota(jnp.int32, (MOBA_BLOCK, MOBA_BLOCK), 1)
    causal = ki_idx <= qi_idx

    q, k_aug, v_aug, q_aug, offsets = [], [], [], [], []
    for h in range(heads):
        pair, second = divmod(h, 2)
        lo = pair * 2 * A_HEAD_DIM
        own = ~first if second else first
        offset = 0 if second else half
        bias_lanes = (lane >= offset) & (lane < offset + nb)
        onehot = (lane == key_block + offset).astype(BF16)
        zero = jnp.zeros((seq, LANES), BF16)
        q1 = jnp.where(own, q_ref[:, lo:lo + LANES], zero)
        q2 = jnp.where(own, q_ref[:, lo + LANES:lo + 2 * LANES], zero)
        q.append((q1, q2))
        k_aug.append(jnp.concatenate([jnp.where(bias_lanes, onehot, k_ref[:, lo:lo + LANES]),
                                      k_ref[:, lo + LANES:lo + 2 * LANES]], axis=1))
        v_aug.append(jnp.concatenate([v_ref[:, h * A_HEAD_DIM:(h + 1) * A_HEAD_DIM], ones], axis=1))
        offsets.append(offset)

    def scores(i, h):
        rows = slice(i * MOBA_BLOCK, (i + 1) * MOBA_BLOCK)
        if i == 0:
            qa = jnp.concatenate([q[h][0][rows, :], q[h][1][rows, :]], axis=1)
        else:
            qa = q_aug[h][rows, :]
        return _dot_nt(qa, k_aug[h][:(i + 1) * MOBA_BLOCK, :])

    def finish(i, h, s):
        s_diag = jnp.where(causal, s[:, i * MOBA_BLOCK:], NEG)
        s = s_diag if i == 0 else jnp.concatenate([s[:, :i * MOBA_BLOCK], s_diag], axis=1)
        m = jnp.max(s, axis=-1, keepdims=True)
        p = jnp.exp2(s - m).astype(BF16)
        o = _dot(p, v_aug[h][:(i + 1) * MOBA_BLOCK, :])
        o_ref[i * MOBA_BLOCK:(i + 1) * MOBA_BLOCK, h * A_HEAD_DIM:(h + 1) * A_HEAD_DIM] = (
            o[:, :A_HEAD_DIM] / o[:, A_HEAD_DIM:A_HEAD_DIM + 1]).astype(BF16)

    tasks = [(i, h) for i in range(nb) for h in range(heads)]
    pending = [scores(0, h) for h in range(heads)]
    for pair in range(heads // 2):
        cols = slice(pair * 2 * A_HEAD_DIM, (pair + 1) * 2 * A_HEAD_DIM)
        gates = _pair_gates(q_ref[:, cols], k_ref[:, cols], seq)
        for h in (2 * pair, 2 * pair + 1):
            bias = _moba_bias(gates[h % 2], seq, offsets[h])
            q_aug.append(jnp.concatenate([q[h][0] + bias, q[h][1]], axis=1))
    for t, (i, h) in enumerate(tasks):
        if t + heads < len(tasks):
            pending.append(scores(*tasks[t + heads]))
        finish(i, h, pending[t])


MOBA_HEADS_PER_STEP = 4


def _moba_attn(q, k, v, batch, seq):
    hps = MOBA_HEADS_PER_STEP
    spec = pl.BlockSpec((seq, hps * A_HEAD_DIM), lambda b, h: (b, h))
    return pl.pallas_call(
        functools.partial(_moba_attn_kernel, seq=seq, heads=hps),
        grid=(batch, A_HEADS // hps),
        in_specs=[spec, spec, spec],
        out_specs=spec,
        out_shape=jax.ShapeDtypeStruct(q.shape, BF16),
        compiler_params=pltpu.CompilerParams(dimension_semantics=("arbitrary", "arbitrary"),
                                             vmem_limit_bytes=VMEM_LIMIT),
        name="moba_attn",
    )(q, k, v)


WEIGHT_CAST_STEPS = 8


def _cast_slab(src_ref, dst_ref, step):
    rows = src_ref.shape[0]
    dst_ref[pl.ds(pl.multiple_of(step * rows, rows), rows), :] = src_ref[...].astype(BF16)


def _slab_spec(stacked, layer):
    _, rows, cols = stacked.shape
    return pl.BlockSpec((None, rows // WEIGHT_CAST_STEPS, cols),
                        lambda s: (layer, jnp.minimum(s, WEIGHT_CAST_STEPS - 1), 0))


def _layer_row_spec(cols, layer):
    return pl.BlockSpec((None, 1, cols), lambda *_: (layer, 0, 0), pipeline_mode=pl.Buffered(1))


def _row_spec(cols):
    return pl.BlockSpec((ROW_TILE, cols), lambda s: (jnp.maximum(s - WEIGHT_CAST_STEPS, 0), 0))


def _mix_mlp_kernel(*refs, gated):
    if gated:
        a_ref, gate_ref, wo_f_ref, x_ref, g_ref, w1_f_ref, w2_f_ref, o_ref, wo_ref, w1_ref, w2_ref, u_ref = refs
    else:
        a_ref, wo_f_ref, x_ref, g_ref, w1_f_ref, w2_f_ref, o_ref, wo_ref, w1_ref, w2_ref, u_ref = refs
    step = pl.program_id(0)

    @pl.when(step < WEIGHT_CAST_STEPS)
    def _cast():
        _cast_slab(wo_f_ref, wo_ref, step)
        _cast_slab(w1_f_ref, w1_ref, step)
        _cast_slab(w2_f_ref, w2_ref, step)

    @pl.when(step >= WEIGHT_CAST_STEPS)
    def _compute():
        a = a_ref[...]
        if gated:
            a = (a.astype(F32) * gate_ref[...].astype(F32)).astype(BF16)
        x1 = x_ref[...] + _dot(a, wo_ref[...])
        h, row_scale = _rms_split(x1, g_ref[...])
        for c in range(D_FF // D_MODEL):
            cols = slice(c * D_MODEL, (c + 1) * D_MODEL)
            u = jnp.maximum(_dot(h, w1_ref[:, cols]) * row_scale, 0.0)
            u_ref[:, cols] = (u * u).astype(BF16)
        o_ref[...] = x1 + _dot(u_ref[...], w2_ref[...])


def _mix_mlp(a, gate, wo, wo_layer, x2, g, w1, w2, layer):
    t, kdim = a.shape
    gated = gate is not None
    acts = [a, gate] if gated else [a]
    return pl.pallas_call(
        functools.partial(_mix_mlp_kernel, gated=gated),
        grid=(WEIGHT_CAST_STEPS + t // ROW_TILE,),
        in_specs=[_row_spec(kdim)] * len(acts) + [
            _slab_spec(wo, wo_layer), _row_spec(D_MODEL), _layer_row_spec(D_MODEL, layer),
            _slab_spec(w1, layer), _slab_spec(w2, layer)],
        out_specs=_row_spec(D_MODEL),
        out_shape=jax.ShapeDtypeStruct(x2.shape, F32),
        scratch_shapes=[pltpu.VMEM(wo.shape[1:], BF16), pltpu.VMEM(w1.shape[1:], BF16),
                        pltpu.VMEM(w2.shape[1:], BF16), pltpu.VMEM((ROW_TILE, D_FF), BF16)],
        compiler_params=_params(),
        name="mix_mlp",
    )(*acts, wo, x2, g, w1, w2)


RET_PROJ_CHUNK = 512
RET_CHUNK = 256


def _ret_proj_kernel(x_ref, g_ref, w_f_ref, cos_ref, sin_ref, lg_ref, q_ref, k_ref, v_ref, sg_ref, w_ref):
    step = pl.program_id(0)
    half = R_QK_DIM // 2
    chunk = RET_PROJ_CHUNK
    n_chunks = 6 * D_MODEL // chunk

    @pl.when(step < WEIGHT_CAST_STEPS)
    def _cast():
        _cast_slab(w_f_ref, w_ref, step)

    @pl.when(step >= WEIGHT_CAST_STEPS)
    def _compute():
        h, row_scale = _rms_split(x_ref[...], g_ref[...])

        def rope_store(y, lo, out_ref, sign, scale):
            local = (lax.broadcasted_iota(jnp.int32, (ROW_TILE, 1), 0) % RET_CHUNK).astype(F32)
            for j in range(chunk // R_QK_DIM):
                hh = lo // R_QK_DIM + j
                decay = jnp.exp(lg_ref[hh] * (sign * local)) * (row_scale * scale)
                c, s = cos_ref[...] * decay, sin_ref[...] * decay
                x1 = y[:, j * R_QK_DIM:j * R_QK_DIM + half]
                x2 = y[:, j * R_QK_DIM + half:(j + 1) * R_QK_DIM]
                out_ref[:, hh * R_QK_DIM:hh * R_QK_DIM + half] = (x1 * c - x2 * s).astype(BF16)
                out_ref[:, hh * R_QK_DIM + half:(hh + 1) * R_QK_DIM] = (x2 * c + x1 * s).astype(BF16)

        def epilogue(c, y):
            col = c * chunk
            if col < D_MODEL:
                rope_store(y, col, q_ref, 1.0, 1.0)
            elif col < 2 * D_MODEL:
                rope_store(y, col - D_MODEL, k_ref, -1.0, R_QK_DIM ** -0.5)
            elif col < 4 * D_MODEL:
                v_ref[:, col - 2 * D_MODEL:col - 2 * D_MODEL + chunk] = (y * row_scale).astype(BF16)
            else:
                y = y * row_scale
                sg_ref[:, col - 4 * D_MODEL:col - 4 * D_MODEL + chunk] = (y * jax.nn.sigmoid(y)).astype(BF16)

        def project(c):
            return _dot(h, w_ref[:, c * chunk:(c + 1) * chunk])

        pending = project(0)
        for c in range(n_chunks):
            following = project(c + 1) if c + 1 < n_chunks else None
            epilogue(c, pending)
            pending = following


def _position_major(batch, seq, tile):
    tiles_per_seq = seq // tile

    def split(s):
        p, b = jnp.divmod(jnp.maximum(s - WEIGHT_CAST_STEPS, 0), batch)
        return p, b * tiles_per_seq + p

    return split


def _ret_proj(x2, g, g_layer, w, w_layer, cos, sin, log_g, batch, seq):
    t = x2.shape[0]
    half = R_QK_DIM // 2
    split = _position_major(batch, seq, ROW_TILE)
    rows = lambda cols: pl.BlockSpec((ROW_TILE, cols), lambda s: (split(s)[1], 0))
    table = pl.BlockSpec((ROW_TILE, half), lambda s: (split(s)[0], 0))
    return pl.pallas_call(
        _ret_proj_kernel,
        grid=(WEIGHT_CAST_STEPS + t // ROW_TILE,),
        in_specs=[rows(D_MODEL), _layer_row_spec(D_MODEL, g_layer), _slab_spec(w, w_layer),
                  table, table, _resident((R_HEADS, 1, 1))],
        out_specs=[rows(D_MODEL), rows(D_MODEL), rows(2 * D_MODEL), rows(2 * D_MODEL)],
        out_shape=[jax.ShapeDtypeStruct((t, D_MODEL), BF16), jax.ShapeDtypeStruct((t, D_MODEL), BF16),
                   jax.ShapeDtypeStruct((t, 2 * D_MODEL), BF16), jax.ShapeDtypeStruct((t, 2 * D_MODEL), BF16)],
        scratch_shapes=[pltpu.VMEM(w.shape[1:], BF16)],
        compiler_params=_params(),
        name="ret_proj",
    )(x2, g, w, cos, sin, log_g)


RET_HEADS_PER_STEP = 2


def _retention_kernel(q_ref, k_ref, v_ref, gch_ref, o_ref, *, seq, heads):
    c = RET_CHUNK
    n_chunks = seq // c
    ti = lax.broadcasted_iota(jnp.int32, (c, c), 0)
    ui = lax.broadcasted_iota(jnp.int32, (c, c), 1)
    causal = ui <= ti

    def chunk_matmuls(hh, n):
        rows = slice(n * c, (n + 1) * c)
        q = q_ref[rows, hh * R_QK_DIM:(hh + 1) * R_QK_DIM]
        k = k_ref[rows, hh * R_QK_DIM:(hh + 1) * R_QK_DIM]
        v = v_ref[rows, hh * R_V_DIM:(hh + 1) * R_V_DIM]
        return _dot_nt(q, k), _dot_tn(k, v)

    tasks = [(hh, n) for hh in range(heads) for n in range(n_chunks)]
    pending = chunk_matmuls(*tasks[0])
    decayed = None
    for t, (hh, n) in enumerate(tasks):
        rows = slice(n * c, (n + 1) * c)
        vcols = slice(hh * R_V_DIM, (hh + 1) * R_V_DIM)
        following = chunk_matmuls(*tasks[t + 1]) if t + 1 < len(tasks) else None
        s, update = pending
        y = _dot(jnp.where(causal, s, 0.0).astype(BF16), v_ref[rows, vcols])
        if n > 0:
            y = y + _dot(q_ref[rows, hh * R_QK_DIM:(hh + 1) * R_QK_DIM], decayed.astype(BF16))
        if n + 1 < n_chunks:
            decayed = (update if n == 0 else decayed + update) * gch_ref[hh]
        o_ref[rows, vcols] = (y * lax.rsqrt(jnp.mean(y * y, axis=-1, keepdims=True) + EPS)).astype(BF16)
        pending = following


def _retention(q, k, v, gch, batch, seq):
    hps = RET_HEADS_PER_STEP
    qk_spec = pl.BlockSpec((seq, hps * R_QK_DIM), lambda b, h: (b, h))
    v_spec = pl.BlockSpec((seq, hps * R_V_DIM), lambda b, h: (b, h))
    return pl.pallas_call(
        functools.partial(_retention_kernel, seq=seq, heads=hps),
        grid=(batch, R_HEADS // hps),
        in_specs=[qk_spec, qk_spec, v_spec, pl.BlockSpec((hps, 1, 1), lambda b, h: (h, 0, 0))],
        out_specs=v_spec,
        out_shape=jax.ShapeDtypeStruct(v.shape, BF16),
        compiler_params=pltpu.CompilerParams(dimension_semantics=("arbitrary", "arbitrary"),
                                             vmem_limit_bytes=VMEM_LIMIT),
        name="retention",
    )(q, k, v, gch)


def _rope_tables(seq, half):
    inv = ROPE_THETA ** (-jnp.arange(half, dtype=F32) / half)
    ang = jnp.arange(seq, dtype=jnp.int32).astype(F32)[:, None] * inv[None, :]
    return jnp.cos(ang), jnp.sin(ang)


def _retention_decay():
    log_g = jnp.log1p(-(2.0 ** (-5.0 - jnp.arange(R_HEADS, dtype=F32))))
    return log_g[:, None, None], jnp.exp(log_g * RET_CHUNK)[:, None, None]


def kernel(x, norm_mix_g, norm_mlp_g, a_w_qkv, a_q_gain, a_k_gain, a_w_o, r_w_in, r_w_out, mlp_w1, mlp_w2):
    batch, seq, d = x.shape
    assert d == D_MODEL and seq % ROW_TILE == 0 and seq % QKV_ROW_TILE == 0 and seq % MOBA_BLOCK == 0
    assert ROW_TILE % RET_CHUNK == 0
    depth = norm_mix_g.shape[0]
    x2 = x.reshape(batch * seq, d)
    norm_mix_g = norm_mix_g.reshape(depth, 1, d)
    norm_mlp_g = norm_mlp_g.reshape(depth, 1, d)

    cos_a, sin_a = _rope_tables(seq, A_HEAD_DIM // 2)
    cos_a = jnp.concatenate([cos_a, cos_a], axis=1)
    sin_a = jnp.concatenate([sin_a, sin_a], axis=1)
    cos_r, sin_r = _rope_tables(seq, R_QK_DIM // 2)
    log_g, gch = _retention_decay()

    a_q_gain = a_q_gain.reshape(-1, 1, A_HEAD_DIM)
    a_k_gain = a_k_gain.reshape(-1, 1, A_HEAD_DIM)

    for i in range(depth):
        j = i // 2
        if i % 2 == 0:
            q, k, v = _moba_qkv(x2, norm_mix_g, i, a_w_qkv, j, a_q_gain, a_k_gain, cos_a, sin_a, batch, seq)
            a, gate, w_o = _moba_attn(q, k, v, batch, seq), None, a_w_o
        else:
            q, k, v, gate = _ret_proj(x2, norm_mix_g, i, r_w_in, j, cos_r, sin_r, log_g, batch, seq)
            a, w_o = _retention(q, k, v, gch, batch, seq), r_w_out
        x2 = _mix_mlp(a, gate, w_o, j, x2, norm_mlp_g, mlp_w1, mlp_w2, i)
    return x2.reshape(batch, seq, d)
```

```python
import functools

import jax
import jax.numpy as jnp
import numpy as np
from jax import lax
from jax.experimental import pallas as pl
from jax.experimental.pallas import tpu as pltpu

D_MODEL = 1024
A_HEADS = 8
A_HEAD_DIM = D_MODEL // A_HEADS
MOBA_BLOCK = 256
MOBA_TOPK = 3
R_HEADS = 4
R_QK_DIM = D_MODEL // R_HEADS
R_V_DIM = 2 * D_MODEL // R_HEADS
D_FF = 4 * D_MODEL
ROPE_THETA = 10000.0
EPS = 1e-6
NEG = -1e30

LANES = 128
ROW_TILE = 512
QKV_ROW_TILE = 1024
VMEM_LIMIT = 56 * 1024 * 1024

BF16 = jnp.bfloat16
F32 = jnp.float32


def _dot(a, b):
    return jnp.dot(a, b, preferred_element_type=F32)


def _dot_nt(a, b):
    return lax.dot_general(a, b, (((1,), (1,)), ((), ())), preferred_element_type=F32)


def _dot_tn(a, b):
    return lax.dot_general(a, b, (((0,), (0,)), ((), ())), preferred_element_type=F32)


def _rms(x, g):
    return x * lax.rsqrt(jnp.mean(x * x, axis=-1, keepdims=True) + EPS) * g


def _rms_split(x, g):
    r = lax.rsqrt(jnp.mean(x * x, axis=-1, keepdims=True) + EPS)
    return (x * g).astype(BF16), r


def _params():
    return pltpu.CompilerParams(dimension_semantics=("arbitrary",), vmem_limit_bytes=VMEM_LIMIT)


def _resident(shape):
    nd = len(shape)
    return pl.BlockSpec(shape, lambda *_: (0,) * nd, pipeline_mode=pl.Buffered(1))


def _pair_halves(x, first):
    half = A_HEAD_DIM // 2
    c0, c1 = x[:, :LANES], x[:, LANES:]
    return (jnp.where(first, c0, pltpu.roll(c1, half, axis=1)),
            jnp.where(first, pltpu.roll(c0, half, axis=1), c1))


def _moba_qkv_kernel(x_ref, g_ref, w_f_ref, qg_ref, kg_ref, cos_ref, sin_ref, q_ref, k_ref, v_ref, w_ref):
    step = pl.program_id(0)
    chunk = 2 * A_HEAD_DIM
    n_chunks = 3 * D_MODEL // chunk

    @pl.when(step < WEIGHT_CAST_STEPS)
    def _cast():
        slab = w_f_ref.shape[0]
        rows = pl.ds(pl.multiple_of(step * slab, slab), slab)
        first = lax.broadcasted_iota(jnp.int32, (slab, LANES), 1) < A_HEAD_DIM // 2
        for c in range(n_chunks):
            cols = slice(c * chunk, (c + 1) * chunk)
            if c * chunk < 2 * D_MODEL:
                a, b = _pair_halves(w_f_ref[:, cols], first)
                w_ref[rows, c * chunk:c * chunk + LANES] = a.astype(BF16)
                w_ref[rows, c * chunk + LANES:(c + 1) * chunk] = b.astype(BF16)
            else:
                w_ref[rows, cols] = w_f_ref[:, cols].astype(BF16)

    @pl.when(step >= WEIGHT_CAST_STEPS)
    def _compute():
        h = _rms(x_ref[...], g_ref[...]).astype(BF16)
        cos = cos_ref[...]
        sin = sin_ref[...]
        first = lax.broadcasted_iota(jnp.int32, (1, LANES), 1) < A_HEAD_DIM // 2

        def rope_tables(gain_ref, scale):
            gain = jnp.broadcast_to(gain_ref[...], (8, A_HEAD_DIM)) * scale
            turned = pltpu.roll(gain, A_HEAD_DIM // 2, axis=1)
            g1 = jnp.where(first, gain, turned)[0:1]
            g2 = jnp.where(first, turned, gain)[0:1]
            return cos * g1, sin * g2, cos * g2, sin * g1

        tables = (rope_tables(qg_ref, A_HEAD_DIM ** -0.5 * float(np.log2(np.e))), rope_tables(kg_ref, 1.0))
        outs = (q_ref, k_ref, v_ref)

        def project(c):
            return _dot(h, w_ref[:, c * chunk:(c + 1) * chunk])

        def head_rsqrt(t, lanes):
            return lax.rsqrt(jnp.sum(jnp.where(lanes, t, 0.0), axis=-1, keepdims=True) * (1.0 / A_HEAD_DIM) + EPS)

        def epilogue(c, y):
            section, lo = divmod(c * chunk, D_MODEL)
            if section == 2:
                v_ref[:, lo:lo + chunk] = y.astype(BF16)
                return
            c1, s2, c2, s1 = tables[section]
            y1 = y[:, :LANES]
            y2 = y[:, LANES:]
            t = y1 * y1 + y2 * y2
            r = jnp.where(first, head_rsqrt(t, first), head_rsqrt(t, ~first))
            outs[section][:, lo:lo + LANES] = ((y1 * c1 - y2 * s2) * r).astype(BF16)
            outs[section][:, lo + LANES:lo + chunk] = ((y2 * c2 + y1 * s1) * r).astype(BF16)

        pending = project(0)
        for c in range(n_chunks):
            following = project(c + 1) if c + 1 < n_chunks else None
            epilogue(c, pending)
            pending = following


def _moba_qkv(x2, g, g_layer, w, w_layer, qg, kg, cos, sin, batch, seq):
    t = x2.shape[0]
    tile = QKV_ROW_TILE
    split = _position_major(batch, seq, tile)
    rows = pl.BlockSpec((tile, D_MODEL), lambda s: (split(s)[1], 0))
    table = pl.BlockSpec((tile, A_HEAD_DIM), lambda s: (split(s)[0], 0))
    out = jax.ShapeDtypeStruct((t, D_MODEL), BF16)
    return pl.pallas_call(
        _moba_qkv_kernel,
        grid=(WEIGHT_CAST_STEPS + t // tile,),
        in_specs=[rows, _layer_row_spec(D_MODEL, g_layer), _slab_spec(w, w_layer),
                  _layer_row_spec(A_HEAD_DIM, w_layer), _layer_row_spec(A_HEAD_DIM, w_layer), table, table],
        out_specs=[rows] * 3,
        out_shape=[out, out, out],
        scratch_shapes=[pltpu.VMEM(w.shape[1:], BF16)],
        compiler_params=_params(),
        name="moba_qkv",
    )(x2, g, w, qg, kg, cos, sin)


def _pair_gates(q_pair, k_pair, seq):
    nb = seq // MOBA_BLOCK
    kmean = jnp.mean(k_pair.astype(F32).reshape(nb, MOBA_BLOCK, 2 * LANES), axis=1)
    first = lax.broadcasted_iota(jnp.int32, (nb, 2 * LANES), 1) % LANES < A_HEAD_DIM // 2
    terms = []
    for own in (first, ~first):
        km = jnp.where(own, kmean, 0.0)
        hi = km.astype(BF16)
        terms += [hi, (km - hi.astype(F32)).astype(BF16)]
    g = _dot_nt(jnp.concatenate(terms, axis=0), q_pair)
    return g[0:nb] + g[nb:2 * nb], g[2 * nb:3 * nb] + g[3 * nb:4 * nb]


def _moba_bias(gate, seq, lane_offset):
    nb = seq // MOBA_BLOCK
    blk = lax.broadcasted_iota(jnp.int32, (nb, seq), 0)
    own = lax.broadcasted_iota(jnp.int32, (nb, seq), 1) // MOBA_BLOCK
    gate = jnp.where(blk < own, gate, NEG)
    rank = jnp.zeros((nb, seq), jnp.int32)
    for j in range(nb):
        gj = gate[j:j + 1, :]
        beats = (gj > gate) | ((gj == gate) & (blk > j))
        rank = rank + beats.astype(jnp.int32)
    chosen = ((rank < MOBA_TOPK) & (blk < own)) | (blk == own)
    bias_t = jnp.where(chosen, 0.0, NEG).astype(F32)
    pad = [jnp.zeros((rows, seq), F32) for rows in (lane_offset, LANES - nb - lane_offset)]
    parts = [part for part in (pad[0], bias_t, pad[1]) if part.shape[0]]
    return jnp.concatenate(parts, axis=0).T.astype(BF16)


def _moba_attn_kernel(q_ref, k_ref, v_ref, o_ref, *, seq, heads):
    nb = seq // MOBA_BLOCK
    half = A_HEAD_DIM // 2
    lane = lax.broadcasted_iota(jnp.int32, (seq, LANES), 1)
    key_block = lax.broadcasted_iota(jnp.int32, (seq, LANES), 0) // MOBA_BLOCK
    first = lane < half
    ones = jnp.ones((seq, LANES), BF16)
    qi_idx = lax.broadcasted_iota(jnp.int32, (MOBA_BLOCK, MOBA_BLOCK), 0)
    ki_idx = lax.broadcasted_iota(jnp.int32, (MOBA_BLOCK, MOBA_BLOCK), 1)
    causal = ki_idx <= qi_idx

    q, k_aug, v_aug, q_aug, offsets = [], [], [], {}, []
    for h in range(heads):
        pair, second = divmod(h, 2)
        lo = pair * 2 * A_HEAD_DIM
        own = ~first if second else first
        offset = 0 if second else half
        bias_lanes = (lane >= offset) & (lane < offset + nb)
        onehot = (lane == key_block + offset).astype(BF16)
        zero = jnp.zeros((seq, LANES), BF16)
        q1 = jnp.where(own, q_ref[:, lo:lo + LANES], zero)
        q2 = jnp.where(own, q_ref[:, lo + LANES:lo + 2 * LANES], zero)
        q.append((q1, q2))
        k_aug.append(jnp.concatenate([jnp.where(bias_lanes, onehot, k_ref[:, lo:lo + LANES]),
                                      k_ref[:, lo + LANES:lo + 2 * LANES]], axis=1))
        v_aug.append(jnp.concatenate([v_ref[:, h * A_HEAD_DIM:(h + 1) * A_HEAD_DIM], ones], axis=1))
        offsets.append(offset)

    def rank_pair(pair):
        cols = slice(pair * 2 * A_HEAD_DIM, (pair + 1) * 2 * A_HEAD_DIM)
        gates = _pair_gates(q_ref[:, cols], k_ref[:, cols], seq)
        for h in (2 * pair, 2 * pair + 1):
            bias = _moba_bias(gates[h % 2], seq, offsets[h])
            q_aug[h] = jnp.concatenate([q[h][0] + bias, q[h][1]], axis=1)

    def scores(i, h):
        rows = slice(i * MOBA_BLOCK, (i + 1) * MOBA_BLOCK)
        if i == 0:
            qa = jnp.concatenate([q[h][0][rows, :], q[h][1][rows, :]], axis=1)
        else:
            qa = q_aug[h][rows, :]
        return _dot_nt(qa, k_aug[h][:(i + 1) * MOBA_BLOCK, :])

    def finish(i, h, s):
        s_diag = jnp.where(causal, s[:, i * MOBA_BLOCK:], NEG)
        s = s_diag if i == 0 else jnp.concatenate([s[:, :i * MOBA_BLOCK], s_diag], axis=1)
        m = jnp.max(s, axis=-1, keepdims=True)
        p = jnp.exp2(s - m).astype(BF16)
        o = _dot(p, v_aug[h][:(i + 1) * MOBA_BLOCK, :])
        o_ref[i * MOBA_BLOCK:(i + 1) * MOBA_BLOCK, h * A_HEAD_DIM:(h + 1) * A_HEAD_DIM] = (
            o[:, :A_HEAD_DIM] / o[:, A_HEAD_DIM:A_HEAD_DIM + 1]).astype(BF16)

    tasks = [(i, h) for i in [0] + list(range(nb - 1, 0, -1)) for h in range(heads)]
    pending = [scores(*task) for task in tasks[:heads]]
    for pair in range(heads // 2):
        rank_pair(pair)
    for t, (i, h) in enumerate(tasks):
        if t + heads < len(tasks):
            pending.append(scores(*tasks[t + heads]))
        finish(i, h, pending[t])


MOBA_HEADS_PER_STEP = 4


def _moba_attn(q, k, v, batch, seq):
    hps = MOBA_HEADS_PER_STEP
    spec = pl.BlockSpec((seq, hps * A_HEAD_DIM), lambda b, h: (b, h))
    return pl.pallas_call(
        functools.partial(_moba_attn_kernel, seq=seq, heads=hps),
        grid=(batch, A_HEADS // hps),
        in_specs=[spec, spec, spec],
        out_specs=spec,
        out_shape=jax.ShapeDtypeStruct(q.shape, BF16),
        compiler_params=pltpu.CompilerParams(dimension_semantics=("arbitrary", "arbitrary"),
                                             vmem_limit_bytes=VMEM_LIMIT),
        name="moba_attn",
    )(q, k, v)


WEIGHT_CAST_STEPS = 8


def _cast_slab(src_ref, dst_ref, step):
    rows = src_ref.shape[0]
    dst_ref[pl.ds(pl.multiple_of(step * rows, rows), rows), :] = src_ref[...].astype(BF16)


def _slab_spec(stacked, layer):
    _, rows, cols = stacked.shape
    return pl.BlockSpec((None, rows // WEIGHT_CAST_STEPS, cols),
                        lambda s: (layer, jnp.minimum(s, WEIGHT_CAST_STEPS - 1), 0))


def _layer_row_spec(cols, layer):
    return pl.BlockSpec((None, 1, cols), lambda *_: (layer, 0, 0), pipeline_mode=pl.Buffered(1))


def _row_spec(cols):
    return pl.BlockSpec((ROW_TILE, cols), lambda s: (jnp.maximum(s - WEIGHT_CAST_STEPS, 0), 0))


def _mix_mlp_kernel(*refs, gated):
    if gated:
        a_ref, gate_ref, wo_f_ref, x_ref, g_ref, w1_f_ref, w2_f_ref, o_ref, wo_ref, w1_ref, w2_ref, u_ref = refs
    else:
        a_ref, wo_f_ref, x_ref, g_ref, w1_f_ref, w2_f_ref, o_ref, wo_ref, w1_ref, w2_ref, u_ref = refs
    step = pl.program_id(0)

    @pl.when(step < WEIGHT_CAST_STEPS)
    def _cast():
        _cast_slab(wo_f_ref, wo_ref, step)
        _cast_slab(w1_f_ref, w1_ref, step)
        _cast_slab(w2_f_ref, w2_ref, step)

    @pl.when(step >= WEIGHT_CAST_STEPS)
    def _compute():
        a = a_ref[...]
        if gated:
            a = (a.astype(F32) * gate_ref[...].astype(F32)).astype(BF16)
        x1 = x_ref[...] + _dot(a, wo_ref[...])
        h, row_scale = _rms_split(x1, g_ref[...])
        for c in range(D_FF // D_MODEL):
            cols = slice(c * D_MODEL, (c + 1) * D_MODEL)
            u = jnp.maximum(_dot(h, w1_ref[:, cols]) * row_scale, 0.0)
            u_ref[:, cols] = (u * u).astype(BF16)
        o_ref[...] = x1 + _dot(u_ref[...], w2_ref[...])


def _mix_mlp(a, gate, wo, wo_layer, x2, g, w1, w2, layer):
    t, kdim = a.shape
    gated = gate is not None
    acts = [a, gate] if gated else [a]
    return pl.pallas_call(
        functools.partial(_mix_mlp_kernel, gated=gated),
        grid=(WEIGHT_CAST_STEPS + t // ROW_TILE,),
        in_specs=[_row_spec(kdim)] * len(acts) + [
            _slab_spec(wo, wo_layer), _row_spec(D_MODEL), _layer_row_spec(D_MODEL, layer),
            _slab_spec(w1, layer), _slab_spec(w2, layer)],
        out_specs=_row_spec(D_MODEL),
        out_shape=jax.ShapeDtypeStruct(x2.shape, F32),
        scratch_shapes=[pltpu.VMEM(wo.shape[1:], BF16), pltpu.VMEM(w1.shape[1:], BF16),
                        pltpu.VMEM(w2.shape[1:], BF16), pltpu.VMEM((ROW_TILE, D_FF), BF16)],
        compiler_params=_params(),
        name="mix_mlp",
    )(*acts, wo, x2, g, w1, w2)


RET_PROJ_CHUNK = 512
RET_CHUNK = 256


def _ret_proj_kernel(x_ref, g_ref, w_f_ref, cos_ref, sin_ref, lg_ref, q_ref, k_ref, v_ref, sg_ref, w_ref):
    step = pl.program_id(0)
    half = R_QK_DIM // 2
    chunk = RET_PROJ_CHUNK
    n_chunks = 6 * D_MODEL // chunk

    @pl.when(step < WEIGHT_CAST_STEPS)
    def _cast():
        _cast_slab(w_f_ref, w_ref, step)

    @pl.when(step >= WEIGHT_CAST_STEPS)
    def _compute():
        h, row_scale = _rms_split(x_ref[...], g_ref[...])

        def rope_store(y, lo, out_ref, sign, scale):
            local = (lax.broadcasted_iota(jnp.int32, (ROW_TILE, 1), 0) % RET_CHUNK).astype(F32)
            for j in range(chunk // R_QK_DIM):
                hh = lo // R_QK_DIM + j
                decay = jnp.exp(lg_ref[hh] * (sign * local)) * (row_scale * scale)
                c, s = cos_ref[...] * decay, sin_ref[...] * decay
                x1 = y[:, j * R_QK_DIM:j * R_QK_DIM + half]
                x2 = y[:, j * R_QK_DIM + half:(j + 1) * R_QK_DIM]
                out_ref[:, hh * R_QK_DIM:hh * R_QK_DIM + half] = (x1 * c - x2 * s).astype(BF16)
                out_ref[:, hh * R_QK_DIM + half:(hh + 1) * R_QK_DIM] = (x2 * c + x1 * s).astype(BF16)

        def epilogue(c, y):
            col = c * chunk
            if col < D_MODEL:
                rope_store(y, col, q_ref, 1.0, 1.0)
            elif col < 2 * D_MODEL:
                rope_store(y, col - D_MODEL, k_ref, -1.0, R_QK_DIM ** -0.5)
            elif col < 4 * D_MODEL:
                v_ref[:, col - 2 * D_MODEL:col - 2 * D_MODEL + chunk] = (y * row_scale).astype(BF16)
            else:
                y = y * row_scale
                sg_ref[:, col - 4 * D_MODEL:col - 4 * D_MODEL + chunk] = (y * jax.nn.sigmoid(y)).astype(BF16)

        def project(c):
            return _dot(h, w_ref[:, c * chunk:(c + 1) * chunk])

        first_v, first_gate = 2 * D_MODEL // chunk, 4 * D_MODEL // chunk
        order = list(range(first_v)) + list(range(first_gate, n_chunks)) + list(range(first_v, first_gate))
        pending = project(order[0])
        for n, c in enumerate(order):
            following = project(order[n + 1]) if n + 1 < n_chunks else None
            epilogue(c, pending)
            pending = following


def _position_major(batch, seq, tile):
    tiles_per_seq = seq // tile

    def split(s):
        p, b = jnp.divmod(jnp.maximum(s - WEIGHT_CAST_STEPS, 0), batch)
        return p, b * tiles_per_seq + p

    return split


def _ret_proj(x2, g, g_layer, w, w_layer, cos, sin, log_g, batch, seq):
    t = x2.shape[0]
    half = R_QK_DIM // 2
    split = _position_major(batch, seq, ROW_TILE)
    rows = lambda cols: pl.BlockSpec((ROW_TILE, cols), lambda s: (split(s)[1], 0))
    table = pl.BlockSpec((ROW_TILE, half), lambda s: (split(s)[0], 0))
    return pl.pallas_call(
        _ret_proj_kernel,
        grid=(WEIGHT_CAST_STEPS + t // ROW_TILE,),
        in_specs=[rows(D_MODEL), _layer_row_spec(D_MODEL, g_layer), _slab_spec(w, w_layer),
                  table, table, _resident((R_HEADS, 1, 1))],
        out_specs=[rows(D_MODEL), rows(D_MODEL), rows(2 * D_MODEL), rows(2 * D_MODEL)],
        out_shape=[jax.ShapeDtypeStruct((t, D_MODEL), BF16), jax.ShapeDtypeStruct((t, D_MODEL), BF16),
                   jax.ShapeDtypeStruct((t, 2 * D_MODEL), BF16), jax.ShapeDtypeStruct((t, 2 * D_MODEL), BF16)],
        scratch_shapes=[pltpu.VMEM(w.shape[1:], BF16)],
        compiler_params=_params(),
        name="ret_proj",
    )(x2, g, w, cos, sin, log_g)


RET_HEADS_PER_STEP = 2


def _retention_kernel(q_ref, k_ref, v_ref, gch_ref, o_ref, *, seq, heads):
    c = RET_CHUNK
    n_chunks = seq // c
    ti = lax.broadcasted_iota(jnp.int32, (c, c), 0)
    ui = lax.broadcasted_iota(jnp.int32, (c, c), 1)
    causal = ui <= ti

    def chunk_matmuls(hh, n):
        rows = slice(n * c, (n + 1) * c)
        q = q_ref[rows, hh * R_QK_DIM:(hh + 1) * R_QK_DIM]
        k = k_ref[rows, hh * R_QK_DIM:(hh + 1) * R_QK_DIM]
        v = v_ref[rows, hh * R_V_DIM:(hh + 1) * R_V_DIM]
        return _dot_nt(q, k), _dot_tn(k, v)

    tasks = [(hh, n) for hh in range(heads) for n in range(n_chunks)]
    pending = chunk_matmuls(*tasks[0])
    decayed = None
    for t, (hh, n) in enumerate(tasks):
        rows = slice(n * c, (n + 1) * c)
        vcols = slice(hh * R_V_DIM, (hh + 1) * R_V_DIM)
        following = chunk_matmuls(*tasks[t + 1]) if t + 1 < len(tasks) else None
        s, update = pending
        y = _dot(jnp.where(causal, s, 0.0).astype(BF16), v_ref[rows, vcols])
        if n > 0:
            y = y + _dot(q_ref[rows, hh * R_QK_DIM:(hh + 1) * R_QK_DIM], decayed.astype(BF16))
        if n + 1 < n_chunks:
            decayed = (update if n == 0 else decayed + update) * gch_ref[hh]
        o_ref[rows, vcols] = (y * lax.rsqrt(jnp.mean(y * y, axis=-1, keepdims=True) + EPS)).astype(BF16)
        pending = following


def _retention(q, k, v, gch, batch, seq):
    hps = RET_HEADS_PER_STEP
    qk_spec = pl.BlockSpec((seq, hps * R_QK_DIM), lambda b, h: (b, h))
    v_spec = pl.BlockSpec((seq, hps * R_V_DIM), lambda b, h: (b, h))
    return pl.pallas_call(
        functools.partial(_retention_kernel, seq=seq, heads=hps),
        grid=(batch, R_HEADS // hps),
        in_specs=[qk_spec, qk_spec, v_spec, pl.BlockSpec((hps, 1, 1), lambda b, h: (h, 0, 0))],
        out_specs=v_spec,
        out_shape=jax.ShapeDtypeStruct(v.shape, BF16),
        compiler_params=pltpu.CompilerParams(dimension_semantics=("arbitrary", "arbitrary"),
                                             vmem_limit_bytes=VMEM_LIMIT),
        name="retention",
    )(q, k, v, gch)


def _rope_tables(seq, half):
    inv = ROPE_THETA ** (-jnp.arange(half, dtype=F32) / half)
    ang = jnp.arange(seq, dtype=jnp.int32).astype(F32)[:, None] * inv[None, :]
    return jnp.cos(ang), jnp.sin(ang)


def _retention_decay():
    log_g = jnp.log1p(-(2.0 ** (-5.0 - jnp.arange(R_HEADS, dtype=F32))))
    return log_g[:, None, None], jnp.exp(log_g * RET_CHUNK)[:, None, None]


def kernel(x, norm_mix_g, norm_mlp_g, a_w_qkv, a_q_gain, a_k_gain, a_w_o, r_w_in, r_w_out, mlp_w1, mlp_w2):
    batch, seq, d = x.shape
    assert d == D_MODEL and seq % ROW_TILE == 0 and seq % QKV_ROW_TILE == 0 and seq % MOBA_BLOCK == 0
    assert ROW_TILE % RET_CHUNK == 0
    depth = norm_mix_g.shape[0]
    x2 = x.reshape(batch * seq, d)
    norm_mix_g = norm_mix_g.reshape(depth, 1, d)
    norm_mlp_g = norm_mlp_g.reshape(depth, 1, d)

    cos_a, sin_a = _rope_tables(seq, A_HEAD_DIM // 2)
    cos_a = jnp.concatenate([cos_a, cos_a], axis=1)
    sin_a = jnp.concatenate([sin_a, sin_a], axis=1)
    cos_r, sin_r = _rope_tables(seq, R_QK_DIM // 2)
    log_g, gch = _retention_decay()

    a_q_gain = a_q_gain.reshape(-1, 1, A_HEAD_DIM)
    a_k_gain = a_k_gain.reshape(-1, 1, A_HEAD_DIM)

    for i in range(depth):
        j = i // 2
        if i % 2 == 0:
            q, k, v = _moba_qkv(x2, norm_mix_g, i, a_w_qkv, j, a_q_gain, a_k_gain, cos_a, sin_a, batch, seq)
            a, gate, w_o = _moba_attn(q, k, v, batch, seq), None, a_w_o
        else:
            q, k, v, gate = _ret_proj(x2, norm_mix_g, i, r_w_in, j, cos_r, sin_r, log_g, batch, seq)
            a, w_o = _retention(q, k, v, gch, batch, seq), r_w_out
        x2 = _mix_mlp(a, gate, w_o, j, x2, norm_mlp_g, mlp_w1, mlp_w2, i)
    return x2.reshape(batch, seq, d)
```

```python
import functools

import jax
import jax.numpy as jnp
import numpy as np
from jax import lax
from jax.experimental import pallas as pl
from jax.experimental.pallas import tpu as pltpu

D_MODEL = 1024
A_HEADS = 8
A_HEAD_DIM = D_MODEL // A_HEADS
MOBA_BLOCK = 256
MOBA_TOPK = 3
R_HEADS = 4
R_QK_DIM = D_MODEL // R_HEADS
R_V_DIM = 2 * D_MODEL // R_HEADS
D_FF = 4 * D_MODEL
ROPE_THETA = 10000.0
EPS = 1e-6
NEG = -1e30

LANES = 128
ROW_TILE = 512
QKV_ROW_TILE = 1024
VMEM_LIMIT = 56 * 1024 * 1024

BF16 = jnp.bfloat16
F32 = jnp.float32


def _dot(a, b):
    return jnp.dot(a, b, preferred_element_type=F32)


def _dot_nt(a, b):
    return lax.dot_general(a, b, (((1,), (1,)), ((), ())), preferred_element_type=F32)


def _dot_tn(a, b):
    return lax.dot_general(a, b, (((0,), (0,)), ((), ())), preferred_element_type=F32)


def _rms(x, g):
    return x * lax.rsqrt(jnp.mean(x * x, axis=-1, keepdims=True) + EPS) * g


def _rms_split(x, g):
    r = lax.rsqrt(jnp.mean(x * x, axis=-1, keepdims=True) + EPS)
    return (x * g).astype(BF16), r


def _params():
    return pltpu.CompilerParams(dimension_semantics=("arbitrary",), vmem_limit_bytes=VMEM_LIMIT)


def _resident(shape):
    nd = len(shape)
    return pl.BlockSpec(shape, lambda *_: (0,) * nd, pipeline_mode=pl.Buffered(1))


def _pair_halves(x, first):
    half = A_HEAD_DIM // 2
    c0, c1 = x[:, :LANES], x[:, LANES:]
    return (jnp.where(first, c0, pltpu.roll(c1, half, axis=1)),
            jnp.where(first, pltpu.roll(c0, half, axis=1), c1))


def _moba_qkv_kernel(x_ref, g_ref, w_f_ref, qg_ref, kg_ref, cos_ref, sin_ref, q_ref, k_ref, v_ref, w_ref):
    step = pl.program_id(0)
    chunk = 2 * A_HEAD_DIM
    n_chunks = 3 * D_MODEL // chunk

    @pl.when(step < WEIGHT_CAST_STEPS)
    def _cast():
        slab = w_f_ref.shape[0]
        rows = pl.ds(pl.multiple_of(step * slab, slab), slab)
        first = lax.broadcasted_iota(jnp.int32, (slab, LANES), 1) < A_HEAD_DIM // 2
        for c in range(n_chunks):
            cols = slice(c * chunk, (c + 1) * chunk)
            if c * chunk < 2 * D_MODEL:
                a, b = _pair_halves(w_f_ref[:, cols], first)
                w_ref[rows, c * chunk:c * chunk + LANES] = a.astype(BF16)
                w_ref[rows, c * chunk + LANES:(c + 1) * chunk] = b.astype(BF16)
            else:
                w_ref[rows, cols] = w_f_ref[:, cols].astype(BF16)

    @pl.when(step >= WEIGHT_CAST_STEPS)
    def _compute():
        h = _rms(x_ref[...], g_ref[...]).astype(BF16)
        cos = cos_ref[...]
        sin = sin_ref[...]
        first = lax.broadcasted_iota(jnp.int32, (1, LANES), 1) < A_HEAD_DIM // 2

        def rope_tables(gain_ref, scale):
            gain = jnp.broadcast_to(gain_ref[...], (8, A_HEAD_DIM)) * scale
            turned = pltpu.roll(gain, A_HEAD_DIM // 2, axis=1)
            g1 = jnp.where(first, gain, turned)[0:1]
            g2 = jnp.where(first, turned, gain)[0:1]
            return cos * g1, sin * g2, cos * g2, sin * g1

        tables = (rope_tables(qg_ref, A_HEAD_DIM ** -0.5 * float(np.log2(np.e))), rope_tables(kg_ref, 1.0))
        outs = (q_ref, k_ref, v_ref)

        def project(c):
            return _dot(h, w_ref[:, c * chunk:(c + 1) * chunk])

        def head_rsqrt(t, lanes):
            return lax.rsqrt(jnp.sum(jnp.where(lanes, t, 0.0), axis=-1, keepdims=True) * (1.0 / A_HEAD_DIM) + EPS)

        def epilogue(c, y):
            section, lo = divmod(c * chunk, D_MODEL)
            if section == 2:
                v_ref[:, lo:lo + chunk] = y.astype(BF16)
                return
            c1, s2, c2, s1 = tables[section]
            y1 = y[:, :LANES]
            y2 = y[:, LANES:]
            t = y1 * y1 + y2 * y2
            r = jnp.where(first, head_rsqrt(t, first), head_rsqrt(t, ~first))
            outs[section][:, lo:lo + LANES] = ((y1 * c1 - y2 * s2) * r).astype(BF16)
            outs[section][:, lo + LANES:lo + chunk] = ((y2 * c2 + y1 * s1) * r).astype(BF16)

        per_section = D_MODEL // chunk
        order = list(range(per_section))
        for n in range(per_section):
            order += [per_section + n, 2 * per_section + n]
        pending = project(order[0])
        for n, c in enumerate(order):
            following = project(order[n + 1]) if n + 1 < n_chunks else None
            epilogue(c, pending)
            pending = following


def _moba_qkv(x2, g, g_layer, w, w_layer, qg, kg, cos, sin, batch, seq):
    t = x2.shape[0]
    tile = QKV_ROW_TILE
    split = _position_major(batch, seq, tile)
    rows = pl.BlockSpec((tile, D_MODEL), lambda s: (split(s)[1], 0))
    table = pl.BlockSpec((tile, A_HEAD_DIM), lambda s: (split(s)[0], 0))
    out = jax.ShapeDtypeStruct((t, D_MODEL), BF16)
    return pl.pallas_call(
        _moba_qkv_kernel,
        grid=(WEIGHT_CAST_STEPS + t // tile,),
        in_specs=[rows, _layer_row_spec(D_MODEL, g_layer), _slab_spec(w, w_layer),
                  _layer_row_spec(A_HEAD_DIM, w_layer), _layer_row_spec(A_HEAD_DIM, w_layer), table, table],
        out_specs=[rows] * 3,
        out_shape=[out, out, out],
        scratch_shapes=[pltpu.VMEM(w.shape[1:], BF16)],
        compiler_params=_params(),
        name="moba_qkv",
    )(x2, g, w, qg, kg, cos, sin)


def _pair_gates(q_pair, k_pair, seq):
    nb = seq // MOBA_BLOCK
    kmean = jnp.mean(k_pair.astype(F32).reshape(nb, MOBA_BLOCK, 2 * LANES), axis=1)
    first = lax.broadcasted_iota(jnp.int32, (nb, 2 * LANES), 1) % LANES < A_HEAD_DIM // 2
    terms = []
    for own in (first, ~first):
        km = jnp.where(own, kmean, 0.0)
        hi = km.astype(BF16)
        terms += [hi, (km - hi.astype(F32)).astype(BF16)]
    g = _dot_nt(jnp.concatenate(terms, axis=0), q_pair)
    return g[0:nb] + g[nb:2 * nb], g[2 * nb:3 * nb] + g[3 * nb:4 * nb]


def _moba_bias(gate, seq, lane_offset):
    nb = seq // MOBA_BLOCK
    blk = lax.broadcasted_iota(jnp.int32, (nb, seq), 0)
    own = lax.broadcasted_iota(jnp.int32, (nb, seq), 1) // MOBA_BLOCK
    gate = jnp.where(blk < own, gate, NEG)
    rank = jnp.zeros((nb, seq), jnp.int32)
    for j in range(nb):
        gj = gate[j:j + 1, :]
        beats = (gj > gate) | ((gj == gate) & (blk > j))
        rank = rank + beats.astype(jnp.int32)
    chosen = ((rank < MOBA_TOPK) & (blk < own)) | (blk == own)
    bias_t = jnp.where(chosen, 0.0, NEG).astype(F32)
    pad = [jnp.zeros((rows, seq), F32) for rows in (lane_offset, LANES - nb - lane_offset)]
    parts = [part for part in (pad[0], bias_t, pad[1]) if part.shape[0]]
    return jnp.concatenate(parts, axis=0).T.astype(BF16)


def _moba_attn_kernel(q_ref, k_ref, v_ref, o_ref, *, seq, heads):
    nb = seq // MOBA_BLOCK
    half = A_HEAD_DIM // 2
    lane = lax.broadcasted_iota(jnp.int32, (seq, LANES), 1)
    key_block = lax.broadcasted_iota(jnp.int32, (seq, LANES), 0) // MOBA_BLOCK
    first = lane < half
    ones = jnp.ones((seq, LANES), BF16)
    qi_idx = lax.broadcasted_iota(jnp.int32, (MOBA_BLOCK, MOBA_BLOCK), 0)
    ki_idx = lax.broadcasted_iota(jnp.int32, (MOBA_BLOCK, MOBA_BLOCK), 1)
    causal = ki_idx <= qi_idx

    q, k_aug, v_aug, q_aug, offsets = [], [], [], {}, []
    for h in range(heads):
        pair, second = divmod(h, 2)
        lo = pair * 2 * A_HEAD_DIM
        own = ~first if second else first
        offset = 0 if second else half
        bias_lanes = (lane >= offset) & (lane < offset + nb)
        onehot = (lane == key_block + offset).astype(BF16)
        zero = jnp.zeros((seq, LANES), BF16)
        q1 = jnp.where(own, q_ref[:, lo:lo + LANES], zero)
        q2 = jnp.where(own, q_ref[:, lo + LANES:lo + 2 * LANES], zero)
        q.append((q1, q2))
        k_aug.append(jnp.concatenate([jnp.where(bias_lanes, onehot, k_ref[:, lo:lo + LANES]),
                                      k_ref[:, lo + LANES:lo + 2 * LANES]], axis=1))
        v_aug.append(jnp.concatenate([v_ref[:, h * A_HEAD_DIM:(h + 1) * A_HEAD_DIM], ones], axis=1))
        offsets.append(offset)

    def rank_pair(pair):
        cols = slice(pair * 2 * A_HEAD_DIM, (pair + 1) * 2 * A_HEAD_DIM)
        gates = _pair_gates(q_ref[:, cols], k_ref[:, cols], seq)
        for h in (2 * pair, 2 * pair + 1):
            bias = _moba_bias(gates[h % 2], seq, offsets[h])
            q_aug[h] = jnp.concatenate([q[h][0] + bias, q[h][1]], axis=1)

    def scores(i, h):
        rows = slice(i * MOBA_BLOCK, (i + 1) * MOBA_BLOCK)
        if i == 0:
            qa = jnp.concatenate([q[h][0][rows, :], q[h][1][rows, :]], axis=1)
        else:
            qa = q_aug[h][rows, :]
        return _dot_nt(qa, k_aug[h][:(i + 1) * MOBA_BLOCK, :])

    def finish(i, h, s):
        s_diag = jnp.where(causal, s[:, i * MOBA_BLOCK:], NEG)
        s = s_diag if i == 0 else jnp.concatenate([s[:, :i * MOBA_BLOCK], s_diag], axis=1)
        m = jnp.max(s, axis=-1, keepdims=True)
        p = jnp.exp2(s - m).astype(BF16)
        o = _dot(p, v_aug[h][:(i + 1) * MOBA_BLOCK, :])
        o_ref[i * MOBA_BLOCK:(i + 1) * MOBA_BLOCK, h * A_HEAD_DIM:(h + 1) * A_HEAD_DIM] = (
            o[:, :A_HEAD_DIM] / o[:, A_HEAD_DIM:A_HEAD_DIM + 1]).astype(BF16)

    tasks = [(i, h) for i in [0] + list(range(nb - 1, 0, -1)) for h in range(heads)]
    pending = [scores(*task) for task in tasks[:heads]]
    for pair in range(heads // 2):
        rank_pair(pair)
    for t, (i, h) in enumerate(tasks):
        if t + heads < len(tasks):
            pending.append(scores(*tasks[t + heads]))
        finish(i, h, pending[t])


MOBA_HEADS_PER_STEP = 4


def _moba_attn(q, k, v, batch, seq):
    hps = MOBA_HEADS_PER_STEP
    spec = pl.BlockSpec((seq, hps * A_HEAD_DIM), lambda b, h: (b, h))
    return pl.pallas_call(
        functools.partial(_moba_attn_kernel, seq=seq, heads=hps),
        grid=(batch, A_HEADS // hps),
        in_specs=[spec, spec, spec],
        out_specs=spec,
        out_shape=jax.ShapeDtypeStruct(q.shape, BF16),
        compiler_params=pltpu.CompilerParams(dimension_semantics=("arbitrary", "arbitrary"),
                                             vmem_limit_bytes=VMEM_LIMIT),
        name="moba_attn",
    )(q, k, v)


WEIGHT_CAST_STEPS = 8


def _cast_slab(src_ref, dst_ref, step):
    rows = src_ref.shape[0]
    dst_ref[pl.ds(pl.multiple_of(step * rows, rows), rows), :] = src_ref[...].astype(BF16)


def _slab_spec(stacked, layer):
    _, rows, cols = stacked.shape
    return pl.BlockSpec((None, rows // WEIGHT_CAST_STEPS, cols),
                        lambda s: (layer, jnp.minimum(s, WEIGHT_CAST_STEPS - 1), 0))


def _layer_row_spec(cols, layer):
    return pl.BlockSpec((None, 1, cols), lambda *_: (layer, 0, 0), pipeline_mode=pl.Buffered(1))


def _row_spec(cols):
    return pl.BlockSpec((ROW_TILE, cols), lambda s: (jnp.maximum(s - WEIGHT_CAST_STEPS, 0), 0))


def _mix_mlp_kernel(*refs, gated):
    if gated:
        a_ref, gate_ref, wo_f_ref, x_ref, g_ref, w1_f_ref, w2_f_ref, o_ref, wo_ref, w1_ref, w2_ref, u_ref = refs
    else:
        a_ref, wo_f_ref, x_ref, g_ref, w1_f_ref, w2_f_ref, o_ref, wo_ref, w1_ref, w2_ref, u_ref = refs
    step = pl.program_id(0)

    @pl.when(step < WEIGHT_CAST_STEPS)
    def _cast():
        _cast_slab(wo_f_ref, wo_ref, step)
        _cast_slab(w1_f_ref, w1_ref, step)
        _cast_slab(w2_f_ref, w2_ref, step)

    @pl.when(step >= WEIGHT_CAST_STEPS)
    def _compute():
        a = a_ref[...]
        if gated:
            a = (a.astype(F32) * gate_ref[...].astype(F32)).astype(BF16)
        x1 = x_ref[...] + _dot(a, wo_ref[...])
        h, row_scale = _rms_split(x1, g_ref[...])
        for c in range(D_FF // D_MODEL):
            cols = slice(c * D_MODEL, (c + 1) * D_MODEL)
            u = jnp.maximum(_dot(h, w1_ref[:, cols]) * row_scale, 0.0)
            u_ref[:, cols] = (u * u).astype(BF16)
        o_ref[...] = x1 + _dot(u_ref[...], w2_ref[...])


def _mix_mlp(a, gate, wo, wo_layer, x2, g, w1, w2, layer):
    t, kdim = a.shape
    gated = gate is not None
    acts = [a, gate] if gated else [a]
    return pl.pallas_call(
        functools.partial(_mix_mlp_kernel, gated=gated),
        grid=(WEIGHT_CAST_STEPS + t // ROW_TILE,),
        in_specs=[_row_spec(kdim)] * len(acts) + [
            _slab_spec(wo, wo_layer), _row_spec(D_MODEL), _layer_row_spec(D_MODEL, layer),
            _slab_spec(w1, layer), _slab_spec(w2, layer)],
        out_specs=_row_spec(D_MODEL),
        out_shape=jax.ShapeDtypeStruct(x2.shape, F32),
        scratch_shapes=[pltpu.VMEM(wo.shape[1:], BF16), pltpu.VMEM(w1.shape[1:], BF16),
                        pltpu.VMEM(w2.shape[1:], BF16), pltpu.VMEM((ROW_TILE, D_FF), BF16)],
        compiler_params=_params(),
        name="mix_mlp",
    )(*acts, wo, x2, g, w1, w2)


RET_PROJ_CHUNK = 512
RET_CHUNK = 256


def _ret_proj_kernel(x_ref, g_ref, w_f_ref, cos_ref, sin_ref, lg_ref, q_ref, k_ref, v_ref, sg_ref, w_ref):
    step = pl.program_id(0)
    half = R_QK_DIM // 2
    chunk = RET_PROJ_CHUNK
    n_chunks = 6 * D_MODEL // chunk

    @pl.when(step < WEIGHT_CAST_STEPS)
    def _cast():
        _cast_slab(w_f_ref, w_ref, step)

    @pl.when(step >= WEIGHT_CAST_STEPS)
    def _compute():
        h, row_scale = _rms_split(x_ref[...], g_ref[...])

        def rope_store(y, lo, out_ref, sign, scale):
            local = (lax.broadcasted_iota(jnp.int32, (ROW_TILE, 1), 0) % RET_CHUNK).astype(F32)
            for j in range(chunk // R_QK_DIM):
                hh = lo // R_QK_DIM + j
                decay = jnp.exp(lg_ref[hh] * (sign * local)) * (row_scale * scale)
                c, s = cos_ref[...] * decay, sin_ref[...] * decay
                x1 = y[:, j * R_QK_DIM:j * R_QK_DIM + half]
                x2 = y[:, j * R_QK_DIM + half:(j + 1) * R_QK_DIM]
                out_ref[:, hh * R_QK_DIM:hh * R_QK_DIM + half] = (x1 * c - x2 * s).astype(BF16)
                out_ref[:, hh * R_QK_DIM + half:(hh + 1) * R_QK_DIM] = (x2 * c + x1 * s).astype(BF16)

        def epilogue(c, y):
            col = c * chunk
            if col < D_MODEL:
                rope_store(y, col, q_ref, 1.0, 1.0)
            elif col < 2 * D_MODEL:
                rope_store(y, col - D_MODEL, k_ref, -1.0, R_QK_DIM ** -0.5)
            elif col < 4 * D_MODEL:
                v_ref[:, col - 2 * D_MODEL:col - 2 * D_MODEL + chunk] = (y * row_scale).astype(BF16)
            else:
                y = y * row_scale
                sg_ref[:, col - 4 * D_MODEL:col - 4 * D_MODEL + chunk] = (y * jax.nn.sigmoid(y)).astype(BF16)

        def project(c):
            return _dot(h, w_ref[:, c * chunk:(c + 1) * chunk])

        first_v, first_gate = 2 * D_MODEL // chunk, 4 * D_MODEL // chunk
        order = list(range(first_v)) + list(range(first_gate, n_chunks)) + list(range(first_v, first_gate))
        pending = project(order[0])
        for n, c in enumerate(order):
            following = project(order[n + 1]) if n + 1 < n_chunks else None
            epilogue(c, pending)
            pending = following


def _position_major(batch, seq, tile):
    tiles_per_seq = seq // tile

    def split(s):
        p, b = jnp.divmod(jnp.maximum(s - WEIGHT_CAST_STEPS, 0), batch)
        return p, b * tiles_per_seq + p

    return split


def _ret_proj(x2, g, g_layer, w, w_layer, cos, sin, log_g, batch, seq):
    t = x2.shape[0]
    half = R_QK_DIM // 2
    split = _position_major(batch, seq, ROW_TILE)
    rows = lambda cols: pl.BlockSpec((ROW_TILE, cols), lambda s: (split(s)[1], 0))
    table = pl.BlockSpec((ROW_TILE, half), lambda s: (split(s)[0], 0))
    return pl.pallas_call(
        _ret_proj_kernel,
        grid=(WEIGHT_CAST_STEPS + t // ROW_TILE,),
        in_specs=[rows(D_MODEL), _layer_row_spec(D_MODEL, g_layer), _slab_spec(w, w_layer),
                  table, table, _resident((R_HEADS, 1, 1))],
        out_specs=[rows(D_MODEL), rows(D_MODEL), rows(2 * D_MODEL), rows(2 * D_MODEL)],
        out_shape=[jax.ShapeDtypeStruct((t, D_MODEL), BF16), jax.ShapeDtypeStruct((t, D_MODEL), BF16),
                   jax.ShapeDtypeStruct((t, 2 * D_MODEL), BF16), jax.ShapeDtypeStruct((t, 2 * D_MODEL), BF16)],
        scratch_shapes=[pltpu.VMEM(w.shape[1:], BF16)],
        compiler_params=_params(),
        name="ret_proj",
    )(x2, g, w, cos, sin, log_g)


RET_HEADS_PER_STEP = 2


def _retention_kernel(q_ref, k_ref, v_ref, gch_ref, o_ref, *, seq, heads):
    c = RET_CHUNK
    n_chunks = seq // c
    ti = lax.broadcasted_iota(jnp.int32, (c, c), 0)
    ui = lax.broadcasted_iota(jnp.int32, (c, c), 1)
    causal = ui <= ti

    def chunk_matmuls(hh, n):
        rows = slice(n * c, (n + 1) * c)
        q = q_ref[rows, hh * R_QK_DIM:(hh + 1) * R_QK_DIM]
        k = k_ref[rows, hh * R_QK_DIM:(hh + 1) * R_QK_DIM]
        v = v_ref[rows, hh * R_V_DIM:(hh + 1) * R_V_DIM]
        return _dot_nt(q, k), _dot_tn(k, v)

    tasks = [(hh, n) for hh in range(heads) for n in range(n_chunks)]
    pending = chunk_matmuls(*tasks[0])
    decayed = None
    for t, (hh, n) in enumerate(tasks):
        rows = slice(n * c, (n + 1) * c)
        vcols = slice(hh * R_V_DIM, (hh + 1) * R_V_DIM)
        following = chunk_matmuls(*tasks[t + 1]) if t + 1 < len(tasks) else None
        s, update = pending
        y = _dot(jnp.where(causal, s, 0.0).astype(BF16), v_ref[rows, vcols])
        if n > 0:
            y = y + _dot(q_ref[rows, hh * R_QK_DIM:(hh + 1) * R_QK_DIM], decayed.astype(BF16))
        if n + 1 < n_chunks:
            decayed = (update if n == 0 else decayed + update) * gch_ref[hh]
        o_ref[rows, vcols] = (y * lax.rsqrt(jnp.mean(y * y, axis=-1, keepdims=True) + EPS)).astype(BF16)
        pending = following


def _retention(q, k, v, gch, batch, seq):
    hps = RET_HEADS_PER_STEP
    qk_spec = pl.BlockSpec((seq, hps * R_QK_DIM), lambda b, h: (b, h))
    v_spec = pl.BlockSpec((seq, hps * R_V_DIM), lambda b, h: (b, h))
    return pl.pallas_call(
        functools.partial(_retention_kernel, seq=seq, heads=hps),
        grid=(batch, R_HEADS // hps),
        in_specs=[qk_spec, qk_spec, v_spec, pl.BlockSpec((hps, 1, 1), lambda b, h: (h, 0, 0))],
        out_specs=v_spec,
        out_shape=jax.ShapeDtypeStruct(v.shape, BF16),
        compiler_params=pltpu.CompilerParams(dimension_semantics=("arbitrary", "arbitrary"),
                                             vmem_limit_bytes=VMEM_LIMIT),
        name="retention",
    )(q, k, v, gch)


def _rope_tables(seq, half):
    inv = ROPE_THETA ** (-jnp.arange(half, dtype=F32) / half)
    ang = jnp.arange(seq, dtype=jnp.int32).astype(F32)[:, None] * inv[None, :]
    return jnp.cos(ang), jnp.sin(ang)


def _retention_decay():
    log_g = jnp.log1p(-(2.0 ** (-5.0 - jnp.arange(R_HEADS, dtype=F32))))
    return log_g[:, None, None], jnp.exp(log_g * RET_CHUNK)[:, None, None]


def kernel(x, norm_mix_g, norm_mlp_g, a_w_qkv, a_q_gain, a_k_gain, a_w_o, r_w_in, r_w_out, mlp_w1, mlp_w2):
    batch, seq, d = x.shape
    assert d == D_MODEL and seq % ROW_TILE == 0 and seq % QKV_ROW_TILE == 0 and seq % MOBA_BLOCK == 0
    assert ROW_TILE % RET_CHUNK == 0
    depth = norm_mix_g.shape[0]
    x2 = x.reshape(batch * seq, d)
    norm_mix_g = norm_mix_g.reshape(depth, 1, d)
    norm_mlp_g = norm_mlp_g.reshape(depth, 1, d)

    cos_a, sin_a = _rope_tables(seq, A_HEAD_DIM // 2)
    cos_a = jnp.concatenate([cos_a, cos_a], axis=1)
    sin_a = jnp.concatenate([sin_a, sin_a], axis=1)
    cos_r, sin_r = _rope_tables(seq, R_QK_DIM // 2)
    log_g, gch = _retention_decay()

    a_q_gain = a_q_gain.reshape(-1, 1, A_HEAD_DIM)
    a_k_gain = a_k_gain.reshape(-1, 1, A_HEAD_DIM)

    for i in range(depth):
        j = i // 2
        if i % 2 == 0:
            q, k, v = _moba_qkv(x2, norm_mix_g, i, a_w_qkv, j, a_q_gain, a_k_gain, cos_a, sin_a, batch, seq)
            a, gate, w_o = _moba_attn(q, k, v, batch, seq), None, a_w_o
        else:
            q, k, v, gate = _ret_proj(x2, norm_mix_g, i, r_w_in, j, cos_r, sin_r, log_g, batch, seq)
            a, w_o = _retention(q, k, v, gch, batch, seq), r_w_out
        x2 = _mix_mlp(a, gate, w_o, j, x2, norm_mlp_g, mlp_w1, mlp_w2, i)
    return x2.reshape(batch, seq, d)
```

```python
import functools

import jax
import jax.numpy as jnp
import numpy as np
from jax import lax
from jax.experimental import pallas as pl
from jax.experimental.pallas import tpu as pltpu

D_MODEL = 1024
A_HEADS = 8
A_HEAD_DIM = D_MODEL // A_HEADS
MOBA_BLOCK = 256
MOBA_TOPK = 3
R_HEADS = 4
R_QK_DIM = D_MODEL // R_HEADS
R_V_DIM = 2 * D_MODEL // R_HEADS
D_FF = 4 * D_MODEL
ROPE_THETA = 10000.0
EPS = 1e-6
NEG = -1e30

LANES = 128
ROW_TILE = 512
QKV_ROW_TILE = 1024
VMEM_LIMIT = 56 * 1024 * 1024

BF16 = jnp.bfloat16
F32 = jnp.float32


def _dot(a, b):
    return jnp.dot(a, b, preferred_element_type=F32)


def _dot_nt(a, b):
    return lax.dot_general(a, b, (((1,), (1,)), ((), ())), preferred_element_type=F32)


def _dot_tn(a, b):
    return lax.dot_general(a, b, (((0,), (0,)), ((), ())), preferred_element_type=F32)


def _rms(x, g):
    return x * lax.rsqrt(jnp.mean(x * x, axis=-1, keepdims=True) + EPS) * g


def _rms_split(x, g):
    r = lax.rsqrt(jnp.mean(x * x, axis=-1, keepdims=True) + EPS)
    return (x * g).astype(BF16), r


def _params():
    return pltpu.CompilerParams(dimension_semantics=("arbitrary",), vmem_limit_bytes=VMEM_LIMIT)


def _resident(shape):
    nd = len(shape)
    return pl.BlockSpec(shape, lambda *_: (0,) * nd, pipeline_mode=pl.Buffered(1))


def _pair_halves(x, first):
    half = A_HEAD_DIM // 2
    c0, c1 = x[:, :LANES], x[:, LANES:]
    return (jnp.where(first, c0, pltpu.roll(c1, half, axis=1)),
            jnp.where(first, pltpu.roll(c0, half, axis=1), c1))


def _moba_qkv_kernel(x_ref, g_ref, w_f_ref, qg_ref, kg_ref, cos_ref, sin_ref, q_ref, k_ref, v_ref, w_ref):
    step = pl.program_id(0)
    chunk = 2 * A_HEAD_DIM
    n_chunks = 3 * D_MODEL // chunk

    @pl.when(step < WEIGHT_CAST_STEPS)
    def _cast():
        slab = w_f_ref.shape[0]
        rows = pl.ds(pl.multiple_of(step * slab, slab), slab)
        first = lax.broadcasted_iota(jnp.int32, (slab, LANES), 1) < A_HEAD_DIM // 2
        for c in range(n_chunks):
            cols = slice(c * chunk, (c + 1) * chunk)
            if c * chunk < 2 * D_MODEL:
                a, b = _pair_halves(w_f_ref[:, cols], first)
                w_ref[rows, c * chunk:c * chunk + LANES] = a.astype(BF16)
                w_ref[rows, c * chunk + LANES:(c + 1) * chunk] = b.astype(BF16)
            else:
                w_ref[rows, cols] = w_f_ref[:, cols].astype(BF16)

    @pl.when(step >= WEIGHT_CAST_STEPS)
    def _compute():
        h = _rms(x_ref[...], g_ref[...]).astype(BF16)
        cos = cos_ref[...]
        sin = sin_ref[...]
        first = lax.broadcasted_iota(jnp.int32, (1, LANES), 1) < A_HEAD_DIM // 2

        def rope_tables(gain_ref, scale):
            gain = jnp.broadcast_to(gain_ref[...], (8, A_HEAD_DIM)) * scale
            turned = pltpu.roll(gain, A_HEAD_DIM // 2, axis=1)
            g1 = jnp.where(first, gain, turned)[0:1]
            g2 = jnp.where(first, turned, gain)[0:1]
            return cos * g1, sin * g2, cos * g2, sin * g1

        tables = (rope_tables(qg_ref, A_HEAD_DIM ** -0.5 * float(np.log2(np.e))), rope_tables(kg_ref, 1.0))
        outs = (q_ref, k_ref, v_ref)

        def project(c):
            return _dot(h, w_ref[:, c * chunk:(c + 1) * chunk])

        def head_rsqrt(t, lanes):
            return lax.rsqrt(jnp.sum(jnp.where(lanes, t, 0.0), axis=-1, keepdims=True) * (1.0 / A_HEAD_DIM) + EPS)

        def epilogue(c, y):
            section, lo = divmod(c * chunk, D_MODEL)
            if section == 2:
                v_ref[:, lo:lo + chunk] = y.astype(BF16)
                return
            c1, s2, c2, s1 = tables[section]
            y1 = y[:, :LANES]
            y2 = y[:, LANES:]
            t = y1 * y1 + y2 * y2
            r = jnp.where(first, head_rsqrt(t, first), head_rsqrt(t, ~first))
            outs[section][:, lo:lo + LANES] = ((y1 * c1 - y2 * s2) * r).astype(BF16)
            outs[section][:, lo + LANES:lo + chunk] = ((y2 * c2 + y1 * s1) * r).astype(BF16)

        per_section = D_MODEL // chunk
        order = []
        for n in range(per_section):
            order += [n, per_section + n, 2 * per_section + n]
        pending = project(order[0])
        for n, c in enumerate(order):
            following = project(order[n + 1]) if n + 1 < n_chunks else None
            epilogue(c, pending)
            pending = following


def _moba_qkv(x2, g, g_layer, w, w_layer, qg, kg, cos, sin, batch, seq):
    t = x2.shape[0]
    tile = QKV_ROW_TILE
    split = _position_major(batch, seq, tile)
    rows = pl.BlockSpec((tile, D_MODEL), lambda s: (split(s)[1], 0))
    table = pl.BlockSpec((tile, A_HEAD_DIM), lambda s: (split(s)[0], 0))
    out = jax.ShapeDtypeStruct((t, D_MODEL), BF16)
    return pl.pallas_call(
        _moba_qkv_kernel,
        grid=(WEIGHT_CAST_STEPS + t // tile,),
        in_specs=[rows, _layer_row_spec(D_MODEL, g_layer), _slab_spec(w, w_layer),
                  _layer_row_spec(A_HEAD_DIM, w_layer), _layer_row_spec(A_HEAD_DIM, w_layer), table, table],
        out_specs=[rows] * 3,
        out_shape=[out, out, out],
        scratch_shapes=[pltpu.VMEM(w.shape[1:], BF16)],
        compiler_params=_params(),
        name="moba_qkv",
    )(x2, g, w, qg, kg, cos, sin)


def _pair_gates(q_pair, k_pair, seq):
    nb = seq // MOBA_BLOCK
    kmean = jnp.mean(k_pair.astype(F32).reshape(nb, MOBA_BLOCK, 2 * LANES), axis=1)
    first = lax.broadcasted_iota(jnp.int32, (nb, 2 * LANES), 1) % LANES < A_HEAD_DIM // 2
    terms = []
    for own in (first, ~first):
        km = jnp.where(own, kmean, 0.0)
        hi = km.astype(BF16)
        terms += [hi, (km - hi.astype(F32)).astype(BF16)]
    g = _dot_nt(jnp.concatenate(terms, axis=0), q_pair)
    return g[0:nb] + g[nb:2 * nb], g[2 * nb:3 * nb] + g[3 * nb:4 * nb]


def _moba_bias(gate, seq, lane_offset):
    nb = seq // MOBA_BLOCK
    blk = lax.broadcasted_iota(jnp.int32, (nb, seq), 0)
    own = lax.broadcasted_iota(jnp.int32, (nb, seq), 1) // MOBA_BLOCK
    gate = jnp.where(blk < own, gate, NEG)
    rank = jnp.zeros((nb, seq), jnp.int32)
    for j in range(nb):
        gj = gate[j:j + 1, :]
        beats = (gj > gate) | ((gj == gate) & (blk > j))
        rank = rank + beats.astype(jnp.int32)
    chosen = ((rank < MOBA_TOPK) & (blk < own)) | (blk == own)
    bias_t = jnp.where(chosen, 0.0, NEG).astype(F32)
    pad = [jnp.zeros((rows, seq), F32) for rows in (lane_offset, LANES - nb - lane_offset)]
    parts = [part for part in (pad[0], bias_t, pad[1]) if part.shape[0]]
    return jnp.concatenate(parts, axis=0).T.astype(BF16)


def _moba_attn_kernel(q_ref, k_ref, v_ref, o_ref, *, seq, heads):
    nb = seq // MOBA_BLOCK
    half = A_HEAD_DIM // 2
    lane = lax.broadcasted_iota(jnp.int32, (seq, LANES), 1)
    key_block = lax.broadcasted_iota(jnp.int32, (seq, LANES), 0) // MOBA_BLOCK
    first = lane < half
    ones = jnp.ones((seq, LANES), BF16)
    qi_idx = lax.broadcasted_iota(jnp.int32, (MOBA_BLOCK, MOBA_BLOCK), 0)
    ki_idx = lax.broadcasted_iota(jnp.int32, (MOBA_BLOCK, MOBA_BLOCK), 1)
    causal = ki_idx <= qi_idx

    q, k_aug, v_aug, q_aug, offsets = [], [], [], {}, []
    for h in range(heads):
        pair, second = divmod(h, 2)
        lo = pair * 2 * A_HEAD_DIM
        own = ~first if second else first
        offset = 0 if second else half
        bias_lanes = (lane >= offset) & (lane < offset + nb)
        onehot = (lane == key_block + offset).astype(BF16)
        zero = jnp.zeros((seq, LANES), BF16)
        q1 = jnp.where(own, q_ref[:, lo:lo + LANES], zero)
        q2 = jnp.where(own, q_ref[:, lo + LANES:lo + 2 * LANES], zero)
        q.append((q1, q2))
        k_aug.append(jnp.concatenate([jnp.where(bias_lanes, onehot, k_ref[:, lo:lo + LANES]),
                                      k_ref[:, lo + LANES:lo + 2 * LANES]], axis=1))
        v_aug.append(jnp.concatenate([v_ref[:, h * A_HEAD_DIM:(h + 1) * A_HEAD_DIM], ones], axis=1))
        offsets.append(offset)

    def rank_pair(pair):
        cols = slice(pair * 2 * A_HEAD_DIM, (pair + 1) * 2 * A_HEAD_DIM)
        gates = _pair_gates(q_ref[:, cols], k_ref[:, cols], seq)
        for h in (2 * pair, 2 * pair + 1):
            bias = _moba_bias(gates[h % 2], seq, offsets[h])
            q_aug[h] = jnp.concatenate([q[h][0] + bias, q[h][1]], axis=1)

    def scores(i, h):
        rows = slice(i * MOBA_BLOCK, (i + 1) * MOBA_BLOCK)
        if i == 0:
            qa = jnp.concatenate([q[h][0][rows, :], q[h][1][rows, :]], axis=1)
        else:
            qa = q_aug[h][rows, :]
        return _dot_nt(qa, k_aug[h][:(i + 1) * MOBA_BLOCK, :])

    def finish(i, h, s):
        s_diag = jnp.where(causal, s[:, i * MOBA_BLOCK:], NEG)
        s = s_diag if i == 0 else jnp.concatenate([s[:, :i * MOBA_BLOCK], s_diag], axis=1)
        m = jnp.max(s, axis=-1, keepdims=True)
        p = jnp.exp2(s - m).astype(BF16)
        o = _dot(p, v_aug[h][:(i + 1) * MOBA_BLOCK, :])
        o_ref[i * MOBA_BLOCK:(i + 1) * MOBA_BLOCK, h * A_HEAD_DIM:(h + 1) * A_HEAD_DIM] = (
            o[:, :A_HEAD_DIM] / o[:, A_HEAD_DIM:A_HEAD_DIM + 1]).astype(BF16)

    tasks = [(i, h) for i in [0] + list(range(nb - 1, 0, -1)) for h in range(heads)]
    pending = [scores(*task) for task in tasks[:heads]]
    for pair in range(heads // 2):
        rank_pair(pair)
    for t, (i, h) in enumerate(tasks):
        if t + heads < len(tasks):
            pending.append(scores(*tasks[t + heads]))
        finish(i, h, pending[t])


MOBA_HEADS_PER_STEP = 4


def _moba_attn(q, k, v, batch, seq):
    hps = MOBA_HEADS_PER_STEP
    spec = pl.BlockSpec((seq, hps * A_HEAD_DIM), lambda b, h: (b, h))
    return pl.pallas_call(
        functools.partial(_moba_attn_kernel, seq=seq, heads=hps),
        grid=(batch, A_HEADS // hps),
        in_specs=[spec, spec, spec],
        out_specs=spec,
        out_shape=jax.ShapeDtypeStruct(q.shape, BF16),
        compiler_params=pltpu.CompilerParams(dimension_semantics=("arbitrary", "arbitrary"),
                                             vmem_limit_bytes=VMEM_LIMIT),
        name="moba_attn",
    )(q, k, v)


WEIGHT_CAST_STEPS = 8


def _cast_slab(src_ref, dst_ref, step):
    rows = src_ref.shape[0]
    dst_ref[pl.ds(pl.multiple_of(step * rows, rows), rows), :] = src_ref[...].astype(BF16)


def _slab_spec(stacked, layer):
    _, rows, cols = stacked.shape
    return pl.BlockSpec((None, rows // WEIGHT_CAST_STEPS, cols),
                        lambda s: (layer, jnp.minimum(s, WEIGHT_CAST_STEPS - 1), 0))


def _layer_row_spec(cols, layer):
    return pl.BlockSpec((None, 1, cols), lambda *_: (layer, 0, 0), pipeline_mode=pl.Buffered(1))


def _row_spec(cols):
    return pl.BlockSpec((ROW_TILE, cols), lambda s: (jnp.maximum(s - WEIGHT_CAST_STEPS, 0), 0))


def _mix_mlp_kernel(*refs, gated):
    if gated:
        a_ref, gate_ref, wo_f_ref, x_ref, g_ref, w1_f_ref, w2_f_ref, o_ref, wo_ref, w1_ref, w2_ref, u_ref = refs
    else:
        a_ref, wo_f_ref, x_ref, g_ref, w1_f_ref, w2_f_ref, o_ref, wo_ref, w1_ref, w2_ref, u_ref = refs
    step = pl.program_id(0)

    @pl.when(step < WEIGHT_CAST_STEPS)
    def _cast():
        _cast_slab(wo_f_ref, wo_ref, step)
        _cast_slab(w1_f_ref, w1_ref, step)
        _cast_slab(w2_f_ref, w2_ref, step)

    @pl.when(step >= WEIGHT_CAST_STEPS)
    def _compute():
        a = a_ref[...]
        if gated:
            a = (a.astype(F32) * gate_ref[...].astype(F32)).astype(BF16)
        x1 = x_ref[...] + _dot(a, wo_ref[...])
        h, row_scale = _rms_split(x1, g_ref[...])
        for c in range(D_FF // D_MODEL):
            cols = slice(c * D_MODEL, (c + 1) * D_MODEL)
            u = jnp.maximum(_dot(h, w1_ref[:, cols]) * row_scale, 0.0)
            u_ref[:, cols] = (u * u).astype(BF16)
        o_ref[...] = x1 + _dot(u_ref[...], w2_ref[...])


def _mix_mlp(a, gate, wo, wo_layer, x2, g, w1, w2, layer):
    t, kdim = a.shape
    gated = gate is not None
    acts = [a, gate] if gated else [a]
    return pl.pallas_call(
        functools.partial(_mix_mlp_kernel, gated=gated),
        grid=(WEIGHT_CAST_STEPS + t // ROW_TILE,),
        in_specs=[_row_spec(kdim)] * len(acts) + [
            _slab_spec(wo, wo_layer), _row_spec(D_MODEL), _layer_row_spec(D_MODEL, layer),
            _slab_spec(w1, layer), _slab_spec(w2, layer)],
        out_specs=_row_spec(D_MODEL),
        out_shape=jax.ShapeDtypeStruct(x2.shape, F32),
        scratch_shapes=[pltpu.VMEM(wo.shape[1:], BF16), pltpu.VMEM(w1.shape[1:], BF16),
                        pltpu.VMEM(w2.shape[1:], BF16), pltpu.VMEM((ROW_TILE, D_FF), BF16)],
        compiler_params=_params(),
        name="mix_mlp",
    )(*acts, wo, x2, g, w1, w2)


RET_PROJ_CHUNK = 512
RET_CHUNK = 256


def _ret_proj_kernel(x_ref, g_ref, w_f_ref, cos_ref, sin_ref, lg_ref, q_ref, k_ref, v_ref, sg_ref, w_ref):
    step = pl.program_id(0)
    half = R_QK_DIM // 2
    chunk = RET_PROJ_CHUNK
    n_chunks = 6 * D_MODEL // chunk

    @pl.when(step < WEIGHT_CAST_STEPS)
    def _cast():
        _cast_slab(w_f_ref, w_ref, step)

    @pl.when(step >= WEIGHT_CAST_STEPS)
    def _compute():
        h, row_scale = _rms_split(x_ref[...], g_ref[...])

        def rope_store(y, lo, out_ref, sign, scale):
            local = (lax.broadcasted_iota(jnp.int32, (ROW_TILE, 1), 0) % RET_CHUNK).astype(F32)
            for j in range(chunk // R_QK_DIM):
                hh = lo // R_QK_DIM + j
                decay = jnp.exp(lg_ref[hh] * (sign * local)) * (row_scale * scale)
                c, s = cos_ref[...] * decay, sin_ref[...] * decay
                x1 = y[:, j * R_QK_DIM:j * R_QK_DIM + half]
                x2 = y[:, j * R_QK_DIM + half:(j + 1) * R_QK_DIM]
                out_ref[:, hh * R_QK_DIM:hh * R_QK_DIM + half] = (x1 * c - x2 * s).astype(BF16)
                out_ref[:, hh * R_QK_DIM + half:(hh + 1) * R_QK_DIM] = (x2 * c + x1 * s).astype(BF16)

        def epilogue(c, y):
            col = c * chunk
            if col < D_MODEL:
                rope_store(y, col, q_ref, 1.0, 1.0)
            elif col < 2 * D_MODEL:
                rope_store(y, col - D_MODEL, k_ref, -1.0, R_QK_DIM ** -0.5)
            elif col < 4 * D_MODEL:
                v_ref[:, col - 2 * D_MODEL:col - 2 * D_MODEL + chunk] = (y * row_scale).astype(BF16)
            else:
                y = y * row_scale
                sg_ref[:, col - 4 * D_MODEL:col - 4 * D_MODEL + chunk] = (y * jax.nn.sigmoid(y)).astype(BF16)

        def project(c):
            return _dot(h, w_ref[:, c * chunk:(c + 1) * chunk])

        first_v, first_gate = 2 * D_MODEL // chunk, 4 * D_MODEL // chunk
        rotary, values, gates = range(first_v), range(first_v, first_gate), range(first_gate, n_chunks)
        order = []
        for n in range(len(values)):
            order += [rotary[n], gates[n], values[n]]
        pending = project(order[0])
        for n, c in enumerate(order):
            following = project(order[n + 1]) if n + 1 < n_chunks else None
            epilogue(c, pending)
            pending = following


def _position_major(batch, seq, tile):
    tiles_per_seq = seq // tile

    def split(s):
        p, b = jnp.divmod(jnp.maximum(s - WEIGHT_CAST_STEPS, 0), batch)
        return p, b * tiles_per_seq + p

    return split


def _ret_proj(x2, g, g_layer, w, w_layer, cos, sin, log_g, batch, seq):
    t = x2.shape[0]
    half = R_QK_DIM // 2
    split = _position_major(batch, seq, ROW_TILE)
    rows = lambda cols: pl.BlockSpec((ROW_TILE, cols), lambda s: (split(s)[1], 0))
    table = pl.BlockSpec((ROW_TILE, half), lambda s: (split(s)[0], 0))
    return pl.pallas_call(
        _ret_proj_kernel,
        grid=(WEIGHT_CAST_STEPS + t // ROW_TILE,),
        in_specs=[rows(D_MODEL), _layer_row_spec(D_MODEL, g_layer), _slab_spec(w, w_layer),
                  table, table, _resident((R_HEADS, 1, 1))],
        out_specs=[rows(D_MODEL), rows(D_MODEL), rows(2 * D_MODEL), rows(2 * D_MODEL)],
        out_shape=[jax.ShapeDtypeStruct((t, D_MODEL), BF16), jax.ShapeDtypeStruct((t, D_MODEL), BF16),
                   jax.ShapeDtypeStruct((t, 2 * D_MODEL), BF16), jax.ShapeDtypeStruct((t, 2 * D_MODEL), BF16)],
        scratch_shapes=[pltpu.VMEM(w.shape[1:], BF16)],
        compiler_params=_params(),
        name="ret_proj",
    )(x2, g, w, cos, sin, log_g)


RET_HEADS_PER_STEP = 2


def _retention_kernel(q_ref, k_ref, v_ref, gch_ref, o_ref, *, seq, heads):
    c = RET_CHUNK
    n_chunks = seq // c
    ti = lax.broadcasted_iota(jnp.int32, (c, c), 0)
    ui = lax.broadcasted_iota(jnp.int32, (c, c), 1)
    causal = ui <= ti

    def chunk_matmuls(hh, n):
        rows = slice(n * c, (n + 1) * c)
        q = q_ref[rows, hh * R_QK_DIM:(hh + 1) * R_QK_DIM]
        k = k_ref[rows, hh * R_QK_DIM:(hh + 1) * R_QK_DIM]
        v = v_ref[rows, hh * R_V_DIM:(hh + 1) * R_V_DIM]
        return _dot_nt(q, k), _dot_tn(k, v)

    tasks = [(hh, n) for hh in range(heads) for n in range(n_chunks)]
    pending = chunk_matmuls(*tasks[0])
    decayed = None
    for t, (hh, n) in enumerate(tasks):
        rows = slice(n * c, (n + 1) * c)
        vcols = slice(hh * R_V_DIM, (hh + 1) * R_V_DIM)
        following = chunk_matmuls(*tasks[t + 1]) if t + 1 < len(tasks) else None
        s, update = pending
        y = _dot(jnp.where(causal, s, 0.0).astype(BF16), v_ref[rows, vcols])
        if n > 0:
            y = y + _dot(q_ref[rows, hh * R_QK_DIM:(hh + 1) * R_QK_DIM], decayed.astype(BF16))
        if n + 1 < n_chunks:
            decayed = (update if n == 0 else decayed + update) * gch_ref[hh]
        o_ref[rows, vcols] = (y * lax.rsqrt(jnp.mean(y * y, axis=-1, keepdims=True) + EPS)).astype(BF16)
        pending = following


def _retention(q, k, v, gch, batch, seq):
    hps = RET_HEADS_PER_STEP
    qk_spec = pl.BlockSpec((seq, hps * R_QK_DIM), lambda b, h: (b, h))
    v_spec = pl.BlockSpec((seq, hps * R_V_DIM), lambda b, h: (b, h))
    return pl.pallas_call(
        functools.partial(_retention_kernel, seq=seq, heads=hps),
        grid=(batch, R_HEADS // hps),
        in_specs=[qk_spec, qk_spec, v_spec, pl.BlockSpec((hps, 1, 1), lambda b, h: (h, 0, 0))],
        out_specs=v_spec,
        out_shape=jax.ShapeDtypeStruct(v.shape, BF16),
        compiler_params=pltpu.CompilerParams(dimension_semantics=("arbitrary", "arbitrary"),
                                             vmem_limit_bytes=VMEM_LIMIT),
        name="retention",
    )(q, k, v, gch)


def _rope_tables(seq, half):
    inv = ROPE_THETA ** (-jnp.arange(half, dtype=F32) / half)
    ang = jnp.arange(seq, dtype=jnp.int32).astype(F32)[:, None] * inv[None, :]
    return jnp.cos(ang), jnp.sin(ang)


def _retention_decay():
    log_g = jnp.log1p(-(2.0 ** (-5.0 - jnp.arange(R_HEADS, dtype=F32))))
    return log_g[:, None, None], jnp.exp(log_g * RET_CHUNK)[:, None, None]


def kernel(x, norm_mix_g, norm_mlp_g, a_w_qkv, a_q_gain, a_k_gain, a_w_o, r_w_in, r_w_out, mlp_w1, mlp_w2):
    batch, seq, d = x.shape
    assert d == D_MODEL and seq % ROW_TILE == 0 and seq % QKV_ROW_TILE == 0 and seq % MOBA_BLOCK == 0
    assert ROW_TILE % RET_CHUNK == 0
    depth = norm_mix_g.shape[0]
    x2 = x.reshape(batch * seq, d)
    norm_mix_g = norm_mix_g.reshape(depth, 1, d)
    norm_mlp_g = norm_mlp_g.reshape(depth, 1, d)

    cos_a, sin_a = _rope_tables(seq, A_HEAD_DIM // 2)
    cos_a = jnp.concatenate([cos_a, cos_a], axis=1)
    sin_a = jnp.concatenate([sin_a, sin_a], axis=1)
    cos_r, sin_r = _rope_tables(seq, R_QK_DIM // 2)
    log_g, gch = _retention_decay()

    a_q_gain = a_q_gain.reshape(-1, 1, A_HEAD_DIM)
    a_k_gain = a_k_gain.reshape(-1, 1, A_HEAD_DIM)

    for i in range(depth):
        j = i // 2
        if i % 2 == 0:
            q, k, v = _moba_qkv(x2, norm_mix_g, i, a_w_qkv, j, a_q_gain, a_k_gain, cos_a, sin_a, batch, seq)
            a, gate, w_o = _moba_attn(q, k, v, batch, seq), None, a_w_o
        else:
            q, k, v, gate = _ret_proj(x2, norm_mix_g, i, r_w_in, j, cos_r, sin_r, log_g, batch, seq)
            a, w_o = _retention(q, k, v, gch, batch, seq), r_w_out
        x2 = _mix_mlp(a, gate, w_o, j, x2, norm_mlp_g, mlp_w1, mlp_w2, i)
    return x2.reshape(batch, seq, d)
```

```python
import functools

import jax
import jax.numpy as jnp
import numpy as np
from jax import lax
from jax.experimental import pallas as pl
from jax.experimental.pallas import tpu as pltpu

D_MODEL = 1024
A_HEADS = 8
A_HEAD_DIM = D_MODEL // A_HEADS
MOBA_BLOCK = 256
MOBA_TOPK = 3
R_HEADS = 4
R_QK_DIM = D_MODEL // R_HEADS
R_V_DIM = 2 * D_MODEL // R_HEADS
D_FF = 4 * D_MODEL
ROPE_THETA = 10000.0
EPS = 1e-6
NEG = -1e30

LANES = 128
ROW_TILE = 512
QKV_ROW_TILE = 1024
VMEM_LIMIT = 56 * 1024 * 1024

BF16 = jnp.bfloat16
F32 = jnp.float32


def _dot(a, b):
    return jnp.dot(a, b, preferred_element_type=F32)


def _dot_nt(a, b):
    return lax.dot_general(a, b, (((1,), (1,)), ((), ())), preferred_element_type=F32)


def _dot_tn(a, b):
    return lax.dot_general(a, b, (((0,), (0,)), ((), ())), preferred_element_type=F32)


def _rms(x, g):
    return x * lax.rsqrt(jnp.mean(x * x, axis=-1, keepdims=True) + EPS) * g


def _rms_split(x, g):
    r = lax.rsqrt(jnp.mean(x * x, axis=-1, keepdims=True) + EPS)
    return (x * g).astype(BF16), r


def _params():
    return pltpu.CompilerParams(dimension_semantics=("arbitrary",), vmem_limit_bytes=VMEM_LIMIT)


def _resident(shape):
    nd = len(shape)
    return pl.BlockSpec(shape, lambda *_: (0,) * nd, pipeline_mode=pl.Buffered(1))


def _pair_halves(x, first):
    half = A_HEAD_DIM // 2
    c0, c1 = x[:, :LANES], x[:, LANES:]
    return (jnp.where(first, c0, pltpu.roll(c1, half, axis=1)),
            jnp.where(first, pltpu.roll(c0, half, axis=1), c1))


def _moba_qkv_kernel(x_ref, g_ref, w_f_ref, qg_ref, kg_ref, cos_ref, sin_ref, q_ref, k_ref, v_ref, w_ref):
    step = pl.program_id(0)
    chunk = 2 * A_HEAD_DIM
    n_chunks = 3 * D_MODEL // chunk

    @pl.when(step < WEIGHT_CAST_STEPS)
    def _cast():
        slab = w_f_ref.shape[0]
        rows = pl.ds(pl.multiple_of(step * slab, slab), slab)
        first = lax.broadcasted_iota(jnp.int32, (slab, LANES), 1) < A_HEAD_DIM // 2
        for c in range(n_chunks):
            cols = slice(c * chunk, (c + 1) * chunk)
            if c * chunk < 2 * D_MODEL:
                a, b = _pair_halves(w_f_ref[:, cols], first)
                w_ref[rows, c * chunk:c * chunk + LANES] = a.astype(BF16)
                w_ref[rows, c * chunk + LANES:(c + 1) * chunk] = b.astype(BF16)
            else:
                w_ref[rows, cols] = w_f_ref[:, cols].astype(BF16)

    @pl.when(step >= WEIGHT_CAST_STEPS)
    def _compute():
        h = _rms(x_ref[...], g_ref[...]).astype(BF16)
        cos = cos_ref[...]
        sin = sin_ref[...]
        first = lax.broadcasted_iota(jnp.int32, (1, LANES), 1) < A_HEAD_DIM // 2

        def rope_tables(gain_ref, scale):
            gain = jnp.broadcast_to(gain_ref[...], (8, A_HEAD_DIM)) * scale
            turned = pltpu.roll(gain, A_HEAD_DIM // 2, axis=1)
            g1 = jnp.where(first, gain, turned)[0:1]
            g2 = jnp.where(first, turned, gain)[0:1]
            return cos * g1, sin * g2, cos * g2, sin * g1

        tables = (rope_tables(qg_ref, A_HEAD_DIM ** -0.5 * float(np.log2(np.e))), rope_tables(kg_ref, 1.0))
        outs = (q_ref, k_ref, v_ref)

        def project(c):
            return _dot(h, w_ref[:, c * chunk:(c + 1) * chunk])

        def head_rsqrt(t, lanes):
            return lax.rsqrt(jnp.sum(jnp.where(lanes, t, 0.0), axis=-1, keepdims=True) * (1.0 / A_HEAD_DIM) + EPS)

        def epilogue(c, y):
            section, lo = divmod(c * chunk, D_MODEL)
            if section == 2:
                v_ref[:, lo:lo + chunk] = y.astype(BF16)
                return
            c1, s2, c2, s1 = tables[section]
            y1 = y[:, :LANES]
            y2 = y[:, LANES:]
            t = y1 * y1 + y2 * y2
            r = jnp.where(first, head_rsqrt(t, first), head_rsqrt(t, ~first))
            outs[section][:, lo:lo + LANES] = ((y1 * c1 - y2 * s2) * r).astype(BF16)
            outs[section][:, lo + LANES:lo + chunk] = ((y2 * c2 + y1 * s1) * r).astype(BF16)

        per_section = D_MODEL // chunk
        order = list(range(per_section))
        for n in range(per_section):
            order += [per_section + n, 2 * per_section + n]
        pending = project(order[0])
        for n, c in enumerate(order):
            following = project(order[n + 1]) if n + 1 < n_chunks else None
            epilogue(c, pending)
            pending = following


def _moba_qkv(x2, g, g_layer, w, w_layer, qg, kg, cos, sin, batch, seq):
    t = x2.shape[0]
    tile = QKV_ROW_TILE
    split = _position_major(batch, seq, tile)
    rows = pl.BlockSpec((tile, D_MODEL), lambda s: (split(s)[1], 0))
    table = pl.BlockSpec((tile, A_HEAD_DIM), lambda s: (split(s)[0], 0))
    out = jax.ShapeDtypeStruct((t, D_MODEL), BF16)
    return pl.pallas_call(
        _moba_qkv_kernel,
        grid=(WEIGHT_CAST_STEPS + t // tile,),
        in_specs=[rows, _layer_row_spec(D_MODEL, g_layer), _slab_spec(w, w_layer),
                  _layer_row_spec(A_HEAD_DIM, w_layer), _layer_row_spec(A_HEAD_DIM, w_layer), table, table],
        out_specs=[rows] * 3,
        out_shape=[out, out, out],
        scratch_shapes=[pltpu.VMEM(w.shape[1:], BF16)],
        compiler_params=_params(),
        name="moba_qkv",
    )(x2, g, w, qg, kg, cos, sin)


def _pair_gates(q_pair, k_pair, seq):
    nb = seq // MOBA_BLOCK
    kmean = jnp.mean(k_pair.astype(F32).reshape(nb, MOBA_BLOCK, 2 * LANES), axis=1)
    first = lax.broadcasted_iota(jnp.int32, (nb, 2 * LANES), 1) % LANES < A_HEAD_DIM // 2
    terms = []
    for own in (first, ~first):
        km = jnp.where(own, kmean, 0.0)
        hi = km.astype(BF16)
        terms += [hi, (km - hi.astype(F32)).astype(BF16)]
    g = _dot_nt(jnp.concatenate(terms, axis=0), q_pair)
    return g[0:nb] + g[nb:2 * nb], g[2 * nb:3 * nb] + g[3 * nb:4 * nb]


def _moba_bias(gate, seq, lane_offset):
    nb = seq // MOBA_BLOCK
    blk = lax.broadcasted_iota(jnp.int32, (nb, seq), 0)
    own = lax.broadcasted_iota(jnp.int32, (nb, seq), 1) // MOBA_BLOCK
    gate = jnp.where(blk < own, gate, NEG)
    rank = jnp.zeros((nb, seq), jnp.int32)
    for j in range(nb):
        gj = gate[j:j + 1, :]
        beats = (gj > gate) | ((gj == gate) & (blk > j))
        rank = rank + beats.astype(jnp.int32)
    chosen = ((rank < MOBA_TOPK) & (blk < own)) | (blk == own)
    bias_t = jnp.where(chosen, 0.0, NEG).astype(F32)
    pad = [jnp.zeros((rows, seq), F32) for rows in (lane_offset, LANES - nb - lane_offset)]
    parts = [part for part in (pad[0], bias_t, pad[1]) if part.shape[0]]
    return jnp.concatenate(parts, axis=0).T.astype(BF16)


def _moba_attn_kernel(q_ref, k_ref, v_ref, o_ref, *, seq, heads):
    nb = seq // MOBA_BLOCK
    half = A_HEAD_DIM // 2
    lane = lax.broadcasted_iota(jnp.int32, (seq, LANES), 1)
    key_block = lax.broadcasted_iota(jnp.int32, (seq, LANES), 0) // MOBA_BLOCK
    first = lane < half
    ones = jnp.ones((seq, LANES), BF16)
    qi_idx = lax.broadcasted_iota(jnp.int32, (MOBA_BLOCK, MOBA_BLOCK), 0)
    ki_idx = lax.broadcasted_iota(jnp.int32, (MOBA_BLOCK, MOBA_BLOCK), 1)
    causal = ki_idx <= qi_idx

    q, k_aug, v_aug, q_aug, offsets = [], [], [], {}, []
    for h in range(heads):
        pair, second = divmod(h, 2)
        lo = pair * 2 * A_HEAD_DIM
        own = ~first if second else first
        offset = 0 if second else half
        bias_lanes = (lane >= offset) & (lane < offset + nb)
        onehot = (lane == key_block + offset).astype(BF16)
        zero = jnp.zeros((seq, LANES), BF16)
        q1 = jnp.where(own, q_ref[:, lo:lo + LANES], zero)
        q2 = jnp.where(own, q_ref[:, lo + LANES:lo + 2 * LANES], zero)
        q.append((q1, q2))
        k_aug.append(jnp.concatenate([jnp.where(bias_lanes, onehot, k_ref[:, lo:lo + LANES]),
                                      k_ref[:, lo + LANES:lo + 2 * LANES]], axis=1))
        v_aug.append(jnp.concatenate([v_ref[:, h * A_HEAD_DIM:(h + 1) * A_HEAD_DIM], ones], axis=1))
        offsets.append(offset)

    def rank_pair(pair):
        cols = slice(pair * 2 * A_HEAD_DIM, (pair + 1) * 2 * A_HEAD_DIM)
        gates = _pair_gates(q_ref[:, cols], k_ref[:, cols], seq)
        for h in (2 * pair, 2 * pair + 1):
            bias = _moba_bias(gates[h % 2], seq, offsets[h])
            q_aug[h] = jnp.concatenate([q[h][0] + bias, q[h][1]], axis=1)

    def scores(i, h):
        rows = slice(i * MOBA_BLOCK, (i + 1) * MOBA_BLOCK)
        if i == 0:
            qa = jnp.concatenate([q[h][0][rows, :], q[h][1][rows, :]], axis=1)
        else:
            qa = q_aug[h][rows, :]
        return _dot_nt(qa, k_aug[h][:(i + 1) * MOBA_BLOCK, :])

    def finish(i, h, s):
        s_diag = jnp.where(causal, s[:, i * MOBA_BLOCK:], NEG)
        s = s_diag if i == 0 else jnp.concatenate([s[:, :i * MOBA_BLOCK], s_diag], axis=1)
        m = jnp.max(s, axis=-1, keepdims=True)
        p = jnp.exp2(s - m).astype(BF16)
        o = _dot(p, v_aug[h][:(i + 1) * MOBA_BLOCK, :])
        o_ref[i * MOBA_BLOCK:(i + 1) * MOBA_BLOCK, h * A_HEAD_DIM:(h + 1) * A_HEAD_DIM] = (
            o[:, :A_HEAD_DIM] / o[:, A_HEAD_DIM:A_HEAD_DIM + 1]).astype(BF16)

    tasks = [(i, h) for i in [0] + list(range(nb - 1, 0, -1)) for h in range(heads)]
    pending = [scores(*task) for task in tasks[:heads]]
    for pair in range(heads // 2):
        rank_pair(pair)
    for t, (i, h) in enumerate(tasks):
        if t + heads < len(tasks):
            pending.append(scores(*tasks[t + heads]))
        finish(i, h, pending[t])


MOBA_HEADS_PER_STEP = 4


def _moba_attn(q, k, v, batch, seq):
    hps = MOBA_HEADS_PER_STEP
    spec = pl.BlockSpec((seq, hps * A_HEAD_DIM), lambda b, h: (b, h))
    return pl.pallas_call(
        functools.partial(_moba_attn_kernel, seq=seq, heads=hps),
        grid=(batch, A_HEADS // hps),
        in_specs=[spec, spec, spec],
        out_specs=spec,
        out_shape=jax.ShapeDtypeStruct(q.shape, BF16),
        compiler_params=pltpu.CompilerParams(dimension_semantics=("arbitrary", "arbitrary"),
                                             vmem_limit_bytes=VMEM_LIMIT),
        name="moba_attn",
    )(q, k, v)


WEIGHT_CAST_STEPS = 8


def _cast_slab(src_ref, dst_ref, step):
    rows = src_ref.shape[0]
    dst_ref[pl.ds(pl.multiple_of(step * rows, rows), rows), :] = src_ref[...].astype(BF16)


def _slab_spec(stacked, layer):
    _, rows, cols = stacked.shape
    return pl.BlockSpec((None, rows // WEIGHT_CAST_STEPS, cols),
                        lambda s: (layer, jnp.minimum(s, WEIGHT_CAST_STEPS - 1), 0))


def _layer_row_spec(cols, layer):
    return pl.BlockSpec((None, 1, cols), lambda *_: (layer, 0, 0), pipeline_mode=pl.Buffered(1))


def _row_spec(cols):
    return pl.BlockSpec((ROW_TILE, cols), lambda s: (jnp.maximum(s - WEIGHT_CAST_STEPS, 0), 0))


def _mix_mlp_kernel(*refs, gated):
    if gated:
        a_ref, gate_ref, wo_f_ref, x_ref, g_ref, w1_f_ref, w2_f_ref, o_ref, wo_ref, w1_ref, w2_ref, u_ref = refs
    else:
        a_ref, wo_f_ref, x_ref, g_ref, w1_f_ref, w2_f_ref, o_ref, wo_ref, w1_ref, w2_ref, u_ref = refs
    step = pl.program_id(0)

    @pl.when(step < WEIGHT_CAST_STEPS)
    def _cast():
        _cast_slab(wo_f_ref, wo_ref, step)
        _cast_slab(w1_f_ref, w1_ref, step)
        _cast_slab(w2_f_ref, w2_ref, step)

    @pl.when(step >= WEIGHT_CAST_STEPS)
    def _compute():
        a = a_ref[...]
        if gated:
            a = (a.astype(F32) * gate_ref[...].astype(F32)).astype(BF16)
        x1 = x_ref[...] + _dot(a, wo_ref[...])
        h, row_scale = _rms_split(x1, g_ref[...])
        for c in range(D_FF // D_MODEL):
            cols = slice(c * D_MODEL, (c + 1) * D_MODEL)
            u = jnp.maximum(_dot(h, w1_ref[:, cols]) * row_scale, 0.0)
            u_ref[:, cols] = (u * u).astype(BF16)
        o_ref[...] = x1 + _dot(u_ref[...], w2_ref[...])


def _mix_mlp(a, gate, wo, wo_layer, x2, g, w1, w2, layer):
    t, kdim = a.shape
    gated = gate is not None
    acts = [a, gate] if gated else [a]
    return pl.pallas_call(
        functools.partial(_mix_mlp_kernel, gated=gated),
        grid=(WEIGHT_CAST_STEPS + t // ROW_TILE,),
        in_specs=[_row_spec(kdim)] * len(acts) + [
            _slab_spec(wo, wo_layer), _row_spec(D_MODEL), _layer_row_spec(D_MODEL, layer),
            _slab_spec(w1, layer), _slab_spec(w2, layer)],
        out_specs=_row_spec(D_MODEL),
        out_shape=jax.ShapeDtypeStruct(x2.shape, F32),
        scratch_shapes=[pltpu.VMEM(wo.shape[1:], BF16), pltpu.VMEM(w1.shape[1:], BF16),
                        pltpu.VMEM(w2.shape[1:], BF16), pltpu.VMEM((ROW_TILE, D_FF), BF16)],
        compiler_params=_params(),
        name="mix_mlp",
    )(*acts, wo, x2, g, w1, w2)


RET_PROJ_CHUNK = 512
RET_CHUNK = 256


def _ret_proj_kernel(x_ref, g_ref, w_f_ref, cos_ref, sin_ref, lg_ref, q_ref, k_ref, v_ref, sg_ref, w_ref):
    step = pl.program_id(0)
    half = R_QK_DIM // 2
    chunk = RET_PROJ_CHUNK
    n_chunks = 6 * D_MODEL // chunk

    @pl.when(step < WEIGHT_CAST_STEPS)
    def _cast():
        _cast_slab(w_f_ref, w_ref, step)

    @pl.when(step >= WEIGHT_CAST_STEPS)
    def _compute():
        h, row_scale = _rms_split(x_ref[...], g_ref[...])

        def rope_store(y, lo, out_ref, sign, scale):
            local = (lax.broadcasted_iota(jnp.int32, (ROW_TILE, 1), 0) % RET_CHUNK).astype(F32)
            for j in range(chunk // R_QK_DIM):
                hh = lo // R_QK_DIM + j
                decay = jnp.exp(lg_ref[hh] * (sign * local)) * (row_scale * scale)
                c, s = cos_ref[...] * decay, sin_ref[...] * decay
                x1 = y[:, j * R_QK_DIM:j * R_QK_DIM + half]
                x2 = y[:, j * R_QK_DIM + half:(j + 1) * R_QK_DIM]
                out_ref[:, hh * R_QK_DIM:hh * R_QK_DIM + half] = (x1 * c - x2 * s).astype(BF16)
                out_ref[:, hh * R_QK_DIM + half:(hh + 1) * R_QK_DIM] = (x2 * c + x1 * s).astype(BF16)

        def epilogue(c, y):
            col = c * chunk
            if col < D_MODEL:
                rope_store(y, col, q_ref, 1.0, 1.0)
            elif col < 2 * D_MODEL:
                rope_store(y, col - D_MODEL, k_ref, -1.0, R_QK_DIM ** -0.5)
            elif col < 4 * D_MODEL:
                v_ref[:, col - 2 * D_MODEL:col - 2 * D_MODEL + chunk] = (y * row_scale).astype(BF16)
            else:
                half_y = y * (0.5 * row_scale)
                sg_ref[:, col - 4 * D_MODEL:col - 4 * D_MODEL + chunk] = (
                    half_y + half_y * jnp.tanh(half_y)).astype(BF16)

        def project(c):
            return _dot(h, w_ref[:, c * chunk:(c + 1) * chunk])

        first_v, first_gate = 2 * D_MODEL // chunk, 4 * D_MODEL // chunk
        order = list(range(first_v)) + list(range(first_gate, n_chunks)) + list(range(first_v, first_gate))
        pending = project(order[0])
        for n, c in enumerate(order):
            following = project(order[n + 1]) if n + 1 < n_chunks else None
            epilogue(c, pending)
            pending = following


def _position_major(batch, seq, tile):
    tiles_per_seq = seq // tile

    def split(s):
        p, b = jnp.divmod(jnp.maximum(s - WEIGHT_CAST_STEPS, 0), batch)
        return p, b * tiles_per_seq + p

    return split


def _ret_proj(x2, g, g_layer, w, w_layer, cos, sin, log_g, batch, seq):
    t = x2.shape[0]
    half = R_QK_DIM // 2
    split = _position_major(batch, seq, ROW_TILE)
    rows = lambda cols: pl.BlockSpec((ROW_TILE, cols), lambda s: (split(s)[1], 0))
    table = pl.BlockSpec((ROW_TILE, half), lambda s: (split(s)[0], 0))
    return pl.pallas_call(
        _ret_proj_kernel,
        grid=(WEIGHT_CAST_STEPS + t // ROW_TILE,),
        in_specs=[rows(D_MODEL), _layer_row_spec(D_MODEL, g_layer), _slab_spec(w, w_layer),
                  table, table, _resident((R_HEADS, 1, 1))],
        out_specs=[rows(D_MODEL), rows(D_MODEL), rows(2 * D_MODEL), rows(2 * D_MODEL)],
        out_shape=[jax.ShapeDtypeStruct((t, D_MODEL), BF16), jax.ShapeDtypeStruct((t, D_MODEL), BF16),
                   jax.ShapeDtypeStruct((t, 2 * D_MODEL), BF16), jax.ShapeDtypeStruct((t, 2 * D_MODEL), BF16)],
        scratch_shapes=[pltpu.VMEM(w.shape[1:], BF16)],
        compiler_params=_params(),
        name="ret_proj",
    )(x2, g, w, cos, sin, log_g)


RET_HEADS_PER_STEP = 2


def _retention_kernel(q_ref, k_ref, v_ref, gch_ref, o_ref, *, seq, heads):
    c = RET_CHUNK
    n_chunks = seq // c
    ti = lax.broadcasted_iota(jnp.int32, (c, c), 0)
    ui = lax.broadcasted_iota(jnp.int32, (c, c), 1)
    causal = ui <= ti

    def chunk_matmuls(hh, n):
        rows = slice(n * c, (n + 1) * c)
        q = q_ref[rows, hh * R_QK_DIM:(hh + 1) * R_QK_DIM]
        k = k_ref[rows, hh * R_QK_DIM:(hh + 1) * R_QK_DIM]
        v = v_ref[rows, hh * R_V_DIM:(hh + 1) * R_V_DIM]
        return _dot_nt(q, k), _dot_tn(k, v)

    tasks = [(hh, n) for hh in range(heads) for n in range(n_chunks)]
    pending = chunk_matmuls(*tasks[0])
    decayed = None
    for t, (hh, n) in enumerate(tasks):
        rows = slice(n * c, (n + 1) * c)
        vcols = slice(hh * R_V_DIM, (hh + 1) * R_V_DIM)
        following = chunk_matmuls(*tasks[t + 1]) if t + 1 < len(tasks) else None
        s, update = pending
        y = _dot(jnp.where(causal, s, 0.0).astype(BF16), v_ref[rows, vcols])
        if n > 0:
            y = y + _dot(q_ref[rows, hh * R_QK_DIM:(hh + 1) * R_QK_DIM], decayed.astype(BF16))
        if n + 1 < n_chunks:
            decayed = (update if n == 0 else decayed + update) * gch_ref[hh]
        o_ref[rows, vcols] = (y * lax.rsqrt(jnp.mean(y * y, axis=-1, keepdims=True) + EPS)).astype(BF16)
        pending = following


def _retention(q, k, v, gch, batch, seq):
    hps = RET_HEADS_PER_STEP
    qk_spec = pl.BlockSpec((seq, hps * R_QK_DIM), lambda b, h: (b, h))
    v_spec = pl.BlockSpec((seq, hps * R_V_DIM), lambda b, h: (b, h))
    return pl.pallas_call(
        functools.partial(_retention_kernel, seq=seq, heads=hps),
        grid=(batch, R_HEADS // hps),
        in_specs=[qk_spec, qk_spec, v_spec, pl.BlockSpec((hps, 1, 1), lambda b, h: (h, 0, 0))],
        out_specs=v_spec,
        out_shape=jax.ShapeDtypeStruct(v.shape, BF16),
        compiler_params=pltpu.CompilerParams(dimension_semantics=("arbitrary", "arbitrary"),
                                             vmem_limit_bytes=VMEM_LIMIT),
        name="retention",
    )(q, k, v, gch)


def _rope_tables(seq, half):
    inv = ROPE_THETA ** (-jnp.arange(half, dtype=F32) / half)
    ang = jnp.arange(seq, dtype=jnp.int32).astype(F32)[:, None] * inv[None, :]
    return jnp.cos(ang), jnp.sin(ang)


def _retention_decay():
    log_g = jnp.log1p(-(2.0 ** (-5.0 - jnp.arange(R_HEADS, dtype=F32))))
    return log_g[:, None, None], jnp.exp(log_g * RET_CHUNK)[:, None, None]


def kernel(x, norm_mix_g, norm_mlp_g, a_w_qkv, a_q_gain, a_k_gain, a_w_o, r_w_in, r_w_out, mlp_w1, mlp_w2):
    batch, seq, d = x.shape
    assert d == D_MODEL and seq % ROW_TILE == 0 and seq % QKV_ROW_TILE == 0 and seq % MOBA_BLOCK == 0
    assert ROW_TILE % RET_CHUNK == 0
    depth = norm_mix_g.shape[0]
    x2 = x.reshape(batch * seq, d)
    norm_mix_g = norm_mix_g.reshape(depth, 1, d)
    norm_mlp_g = norm_mlp_g.reshape(depth, 1, d)

    cos_a, sin_a = _rope_tables(seq, A_HEAD_DIM // 2)
    cos_a = jnp.concatenate([cos_a, cos_a], axis=1)
    sin_a = jnp.concatenate([sin_a, sin_a], axis=1)
    cos_r, sin_r = _rope_tables(seq, R_QK_DIM // 2)
    log_g, gch = _retention_decay()

    a_q_gain = a_q_gain.reshape(-1, 1, A_HEAD_DIM)
    a_k_gain = a_k_gain.reshape(-1, 1, A_HEAD_DIM)

    for i in range(depth):
        j = i // 2
        if i % 2 == 0:
            q, k, v = _moba_qkv(x2, norm_mix_g, i, a_w_qkv, j, a_q_gain, a_k_gain, cos_a, sin_a, batch, seq)
            a, gate, w_o = _moba_attn(q, k, v, batch, seq), None, a_w_o
        else:
            q, k, v, gate = _ret_proj(x2, norm_mix_g, i, r_w_in, j, cos_r, sin_r, log_g, batch, seq)
            a, w_o = _retention(q, k, v, gch, batch, seq), r_w_out
        x2 = _mix_mlp(a, gate, w_o, j, x2, norm_mlp_g, mlp_w1, mlp_w2, i)
    return x2.reshape(batch, seq, d)
```

```python
import functools

import jax
import jax.numpy as jnp
import numpy as np
from jax import lax
from jax.experimental import pallas as pl
from jax.experimental.pallas import tpu as pltpu

D_MODEL = 1024
A_HEADS = 8
A_HEAD_DIM = D_MODEL // A_HEADS
MOBA_BLOCK = 256
MOBA_TOPK = 3
R_HEADS = 4
R_QK_DIM = D_MODEL // R_HEADS
R_V_DIM = 2 * D_MODEL // R_HEADS
D_FF = 4 * D_MODEL
ROPE_THETA = 10000.0
EPS = 1e-6
NEG = -1e30

LANES = 128
ROW_TILE = 512
QKV_ROW_TILE = 1024
VMEM_LIMIT = 56 * 1024 * 1024

BF16 = jnp.bfloat16
F32 = jnp.float32


def _dot(a, b):
    return jnp.dot(a, b, preferred_element_type=F32)


def _dot_nt(a, b):
    return lax.dot_general(a, b, (((1,), (1,)), ((), ())), preferred_element_type=F32)


def _dot_tn(a, b):
    return lax.dot_general(a, b, (((0,), (0,)), ((), ())), preferred_element_type=F32)


def _rms(x, g):
    return x * lax.rsqrt(jnp.mean(x * x, axis=-1, keepdims=True) + EPS) * g


def _rms_split(x, g):
    r = lax.rsqrt(jnp.mean(x * x, axis=-1, keepdims=True) + EPS)
    return (x * g).astype(BF16), r


def _params():
    return pltpu.CompilerParams(dimension_semantics=("arbitrary",), vmem_limit_bytes=VMEM_LIMIT)


def _resident(shape):
    nd = len(shape)
    return pl.BlockSpec(shape, lambda *_: (0,) * nd, pipeline_mode=pl.Buffered(1))


def _pair_halves(x, first):
    half = A_HEAD_DIM // 2
    c0, c1 = x[:, :LANES], x[:, LANES:]
    return (jnp.where(first, c0, pltpu.roll(c1, half, axis=1)),
            jnp.where(first, pltpu.roll(c0, half, axis=1), c1))


def _moba_qkv_kernel(x_ref, g_ref, w_f_ref, qg_ref, kg_ref, cos_ref, sin_ref, q_ref, k_ref, v_ref, w_ref):
    step = pl.program_id(0)
    chunk = 2 * A_HEAD_DIM
    n_chunks = 3 * D_MODEL // chunk

    @pl.when(step < WEIGHT_CAST_STEPS)
    def _cast():
        slab = w_f_ref.shape[0]
        rows = pl.ds(pl.multiple_of(step * slab, slab), slab)
        first = lax.broadcasted_iota(jnp.int32, (slab, LANES), 1) < A_HEAD_DIM // 2
        for c in range(n_chunks):
            cols = slice(c * chunk, (c + 1) * chunk)
            if c * chunk < 2 * D_MODEL:
                a, b = _pair_halves(w_f_ref[:, cols], first)
                w_ref[rows, c * chunk:c * chunk + LANES] = a.astype(BF16)
                w_ref[rows, c * chunk + LANES:(c + 1) * chunk] = b.astype(BF16)
            else:
                w_ref[rows, cols] = w_f_ref[:, cols].astype(BF16)

    @pl.when(step >= WEIGHT_CAST_STEPS)
    def _compute():
        h = _rms(x_ref[...], g_ref[...]).astype(BF16)
        cos = cos_ref[...]
        sin = sin_ref[...]
        first = lax.broadcasted_iota(jnp.int32, (1, LANES), 1) < A_HEAD_DIM // 2

        def rope_tables(gain_ref, scale):
            gain = jnp.broadcast_to(gain_ref[...], (8, A_HEAD_DIM)) * scale
            turned = pltpu.roll(gain, A_HEAD_DIM // 2, axis=1)
            g1 = jnp.where(first, gain, turned)[0:1]
            g2 = jnp.where(first, turned, gain)[0:1]
            return cos * g1, sin * g2, cos * g2, sin * g1

        tables = (rope_tables(qg_ref, A_HEAD_DIM ** -0.5 * float(np.log2(np.e))), rope_tables(kg_ref, 1.0))
        outs = (q_ref, k_ref, v_ref)

        def project(c):
            return _dot(h, w_ref[:, c * chunk:(c + 1) * chunk])

        def head_rsqrt(t, lanes):
            return lax.rsqrt(jnp.sum(jnp.where(lanes, t, 0.0), axis=-1, keepdims=True) * (1.0 / A_HEAD_DIM) + EPS)

        def epilogue(c, y):
            section, lo = divmod(c * chunk, D_MODEL)
            if section == 2:
                v_ref[:, lo:lo + chunk] = y.astype(BF16)
                return
            c1, s2, c2, s1 = tables[section]
            y1 = y[:, :LANES]
            y2 = y[:, LANES:]
            t = y1 * y1 + y2 * y2
            r = jnp.where(first, head_rsqrt(t, first), head_rsqrt(t, ~first))
            outs[section][:, lo:lo + LANES] = ((y1 * c1 - y2 * s2) * r).astype(BF16)
            outs[section][:, lo + LANES:lo + chunk] = ((y2 * c2 + y1 * s1) * r).astype(BF16)

        per_section = D_MODEL // chunk
        order = list(range(per_section))
        for n in range(per_section):
            order += [per_section + n, 2 * per_section + n]
        pending = project(order[0])
        for n, c in enumerate(order):
            following = project(order[n + 1]) if n + 1 < n_chunks else None
            epilogue(c, pending)
            pending = following


def _moba_qkv(x2, g, g_layer, w, w_layer, qg, kg, cos, sin, batch, seq):
    t = x2.shape[0]
    tile = QKV_ROW_TILE
    split = _position_major(batch, seq, tile)
    rows = pl.BlockSpec((tile, D_MODEL), lambda s: (split(s)[1], 0))
    table = pl.BlockSpec((tile, A_HEAD_DIM), lambda s: (split(s)[0], 0))
    out = jax.ShapeDtypeStruct((t, D_MODEL), BF16)
    return pl.pallas_call(
        _moba_qkv_kernel,
        grid=(WEIGHT_CAST_STEPS + t // tile,),
        in_specs=[rows, _layer_row_spec(D_MODEL, g_layer), _slab_spec(w, w_layer),
                  _layer_row_spec(A_HEAD_DIM, w_layer), _layer_row_spec(A_HEAD_DIM, w_layer), table, table],
        out_specs=[rows] * 3,
        out_shape=[out, out, out],
        scratch_shapes=[pltpu.VMEM(w.shape[1:], BF16)],
        compiler_params=_params(),
        name="moba_qkv",
    )(x2, g, w, qg, kg, cos, sin)


def _pair_gates(q_pair, k_pair, seq):
    nb = seq // MOBA_BLOCK
    kmean = jnp.mean(k_pair.astype(F32).reshape(nb, MOBA_BLOCK, 2 * LANES), axis=1)
    first = lax.broadcasted_iota(jnp.int32, (nb, 2 * LANES), 1) % LANES < A_HEAD_DIM // 2
    terms = []
    for own in (first, ~first):
        km = jnp.where(own, kmean, 0.0)
        hi = km.astype(BF16)
        terms += [hi, (km - hi.astype(F32)).astype(BF16)]
    g = _dot_nt(jnp.concatenate(terms, axis=0), q_pair)
    return g[0:nb] + g[nb:2 * nb], g[2 * nb:3 * nb] + g[3 * nb:4 * nb]


def _moba_bias(gate, seq, lane_offset):
    nb = seq // MOBA_BLOCK
    blk = lax.broadcasted_iota(jnp.int32, (nb, seq), 0)
    own = lax.broadcasted_iota(jnp.int32, (nb, seq), 1) // MOBA_BLOCK
    gate = jnp.where(blk < own, gate, NEG)
    rank = jnp.zeros((nb, seq), jnp.int32)
    for j in range(nb):
        gj = gate[j:j + 1, :]
        beats = (gj > gate) | ((gj == gate) & (blk > j))
        rank = rank + beats.astype(jnp.int32)
    chosen = ((rank < MOBA_TOPK) & (blk < own)) | (blk == own)
    bias_t = jnp.where(chosen, 0.0, NEG).astype(F32)
    pad = [jnp.zeros((rows, seq), F32) for rows in (lane_offset, LANES - nb - lane_offset)]
    parts = [part for part in (pad[0], bias_t, pad[1]) if part.shape[0]]
    return jnp.concatenate(parts, axis=0).T.astype(BF16)


def _moba_attn_kernel(q_ref, k_ref, v_ref, o_ref, *, seq, heads):
    nb = seq // MOBA_BLOCK
    half = A_HEAD_DIM // 2
    lane = lax.broadcasted_iota(jnp.int32, (seq, LANES), 1)
    key_block = lax.broadcasted_iota(jnp.int32, (seq, LANES), 0) // MOBA_BLOCK
    first = lane < half
    ones = jnp.ones((seq, LANES), BF16)
    qi_idx = lax.broadcasted_iota(jnp.int32, (MOBA_BLOCK, MOBA_BLOCK), 0)
    ki_idx = lax.broadcasted_iota(jnp.int32, (MOBA_BLOCK, MOBA_BLOCK), 1)
    causal = ki_idx <= qi_idx

    q, k_aug, v_aug, q_aug, offsets = [], [], [], {}, []
    for h in range(heads):
        pair, second = divmod(h, 2)
        lo = pair * 2 * A_HEAD_DIM
        own = ~first if second else first
        offset = 0 if second else half
        bias_lanes = (lane >= offset) & (lane < offset + nb)
        onehot = (lane == key_block + offset).astype(BF16)
        zero = jnp.zeros((seq, LANES), BF16)
        q1 = jnp.where(own, q_ref[:, lo:lo + LANES], zero)
        q2 = jnp.where(own, q_ref[:, lo + LANES:lo + 2 * LANES], zero)
        q.append((q1, q2))
        k_aug.append(jnp.concatenate([jnp.where(bias_lanes, onehot, k_ref[:, lo:lo + LANES]),
                                      k_ref[:, lo + LANES:lo + 2 * LANES]], axis=1))
        v_aug.append(jnp.concatenate([v_ref[:, h * A_HEAD_DIM:(h + 1) * A_HEAD_DIM], ones], axis=1))
        offsets.append(offset)

    def rank_pair(pair):
        cols = slice(pair * 2 * A_HEAD_DIM, (pair + 1) * 2 * A_HEAD_DIM)
        gates = _pair_gates(q_ref[:, cols], k_ref[:, cols], seq)
        for h in (2 * pair, 2 * pair + 1):
            bias = _moba_bias(gates[h % 2], seq, offsets[h])
            q_aug[h] = jnp.concatenate([q[h][0] + bias, q[h][1]], axis=1)

    def scores(i, h):
        rows = slice(i * MOBA_BLOCK, (i + 1) * MOBA_BLOCK)
        if i == 0:
            qa = jnp.concatenate([q[h][0][rows, :], q[h][1][rows, :]], axis=1)
        else:
            qa = q_aug[h][rows, :]
        return _dot_nt(qa, k_aug[h][:(i + 1) * MOBA_BLOCK, :])

    def finish(i, h, s):
        s_diag = jnp.where(causal, s[:, i * MOBA_BLOCK:], NEG)
        s = s_diag if i == 0 else jnp.concatenate([s[:, :i * MOBA_BLOCK], s_diag], axis=1)
        m = jnp.max(s, axis=-1, keepdims=True)
        p = jnp.exp2(s - m).astype(BF16)
        o = _dot(p, v_aug[h][:(i + 1) * MOBA_BLOCK, :])
        o_ref[i * MOBA_BLOCK:(i + 1) * MOBA_BLOCK, h * A_HEAD_DIM:(h + 1) * A_HEAD_DIM] = (
            o[:, :A_HEAD_DIM] / o[:, A_HEAD_DIM:A_HEAD_DIM + 1]).astype(BF16)

    tasks = [(i, h) for i in [0] + list(range(nb - 1, 0, -1)) for h in range(heads)]
    pending = [scores(*task) for task in tasks[:heads]]
    for pair in range(heads // 2):
        rank_pair(pair)
    for t, (i, h) in enumerate(tasks):
        if t + heads < len(tasks):
            pending.append(scores(*tasks[t + heads]))
        finish(i, h, pending[t])


MOBA_HEADS_PER_STEP = 4


def _moba_attn(q, k, v, batch, seq):
    hps = MOBA_HEADS_PER_STEP
    spec = pl.BlockSpec((seq, hps * A_HEAD_DIM), lambda b, h: (b, h))
    return pl.pallas_call(
        functools.partial(_moba_attn_kernel, seq=seq, heads=hps),
        grid=(batch, A_HEADS // hps),
        in_specs=[spec, spec, spec],
        out_specs=spec,
        out_shape=jax.ShapeDtypeStruct(q.shape, BF16),
        compiler_params=pltpu.CompilerParams(dimension_semantics=("arbitrary", "arbitrary"),
                                             vmem_limit_bytes=VMEM_LIMIT),
        name="moba_attn",
    )(q, k, v)


WEIGHT_CAST_STEPS = 8


def _cast_slab(src_ref, dst_ref, step):
    rows = src_ref.shape[0]
    dst_ref[pl.ds(pl.multiple_of(step * rows, rows), rows), :] = src_ref[...].astype(BF16)


def _slab_spec(stacked, layer):
    _, rows, cols = stacked.shape
    return pl.BlockSpec((None, rows // WEIGHT_CAST_STEPS, cols),
                        lambda s: (layer, jnp.minimum(s, WEIGHT_CAST_STEPS - 1), 0))


def _layer_row_spec(cols, layer):
    return pl.BlockSpec((None, 1, cols), lambda *_: (layer, 0, 0), pipeline_mode=pl.Buffered(1))


def _row_spec(cols):
    return pl.BlockSpec((ROW_TILE, cols), lambda s: (jnp.maximum(s - WEIGHT_CAST_STEPS, 0), 0))


def _mix_mlp_kernel(*refs, gated):
    if gated:
        a_ref, gate_ref, wo_f_ref, x_ref, g_ref, w1_f_ref, w2_f_ref, o_ref, wo_ref, w1_ref, w2_ref, u_ref = refs
    else:
        a_ref, wo_f_ref, x_ref, g_ref, w1_f_ref, w2_f_ref, o_ref, wo_ref, w1_ref, w2_ref, u_ref = refs
    step = pl.program_id(0)

    @pl.when(step < WEIGHT_CAST_STEPS)
    def _cast():
        _cast_slab(wo_f_ref, wo_ref, step)
        _cast_slab(w1_f_ref, w1_ref, step)
        _cast_slab(w2_f_ref, w2_ref, step)

    @pl.when(step >= WEIGHT_CAST_STEPS)
    def _compute():
        a = a_ref[...]
        if gated:
            a = (a.astype(F32) * gate_ref[...].astype(F32)).astype(BF16)
        x1 = x_ref[...] + _dot(a, wo_ref[...])
        h, row_scale = _rms_split(x1, g_ref[...])
        for c in range(D_FF // D_MODEL):
            cols = slice(c * D_MODEL, (c + 1) * D_MODEL)
            u = jnp.maximum(_dot(h, w1_ref[:, cols]) * row_scale, 0.0)
            u_ref[:, cols] = (u * u).astype(BF16)
        o_ref[...] = x1 + _dot(u_ref[...], w2_ref[...])


def _mix_mlp(a, gate, wo, wo_layer, x2, g, w1, w2, layer):
    t, kdim = a.shape
    gated = gate is not None
    acts = [a, gate] if gated else [a]
    return pl.pallas_call(
        functools.partial(_mix_mlp_kernel, gated=gated),
        grid=(WEIGHT_CAST_STEPS + t // ROW_TILE,),
        in_specs=[_row_spec(kdim)] * len(acts) + [
            _slab_spec(wo, wo_layer), _row_spec(D_MODEL), _layer_row_spec(D_MODEL, layer),
            _slab_spec(w1, layer), _slab_spec(w2, layer)],
        out_specs=_row_spec(D_MODEL),
        out_shape=jax.ShapeDtypeStruct(x2.shape, F32),
        scratch_shapes=[pltpu.VMEM(wo.shape[1:], BF16), pltpu.VMEM(w1.shape[1:], BF16),
                        pltpu.VMEM(w2.shape[1:], BF16), pltpu.VMEM((ROW_TILE, D_FF), BF16)],
        compiler_params=_params(),
        name="mix_mlp",
    )(*acts, wo, x2, g, w1, w2)


RET_PROJ_CHUNK = 512
RET_CHUNK = 256


def _ret_proj_kernel(x_ref, g_ref, w_f_ref, cos_ref, sin_ref, lg_ref, q_ref, k_ref, v_ref, sg_ref, w_ref):
    step = pl.program_id(0)
    half = R_QK_DIM // 2
    chunk = RET_PROJ_CHUNK
    n_chunks = 6 * D_MODEL // chunk

    @pl.when(step < WEIGHT_CAST_STEPS)
    def _cast():
        _cast_slab(w_f_ref, w_ref, step)

    @pl.when(step >= WEIGHT_CAST_STEPS)
    def _compute():
        h, row_scale = _rms_split(x_ref[...], g_ref[...])

        def rope_store(y, lo, out_ref, sign, scale):
            local = (lax.broadcasted_iota(jnp.int32, (ROW_TILE, half), 0) % RET_CHUNK).astype(F32)
            for j in range(chunk // R_QK_DIM):
                hh = lo // R_QK_DIM + j
                decay = jnp.exp(lg_ref[hh] * (sign * local)) * (row_scale * scale)
                c, s = cos_ref[...] * decay, sin_ref[...] * decay
                x1 = y[:, j * R_QK_DIM:j * R_QK_DIM + half]
                x2 = y[:, j * R_QK_DIM + half:(j + 1) * R_QK_DIM]
                out_ref[:, hh * R_QK_DIM:hh * R_QK_DIM + half] = (x1 * c - x2 * s).astype(BF16)
                out_ref[:, hh * R_QK_DIM + half:(hh + 1) * R_QK_DIM] = (x2 * c + x1 * s).astype(BF16)

        def epilogue(c, y):
            col = c * chunk
            if col < D_MODEL:
                rope_store(y, col, q_ref, 1.0, 1.0)
            elif col < 2 * D_MODEL:
                rope_store(y, col - D_MODEL, k_ref, -1.0, R_QK_DIM ** -0.5)
            elif col < 4 * D_MODEL:
                v_ref[:, col - 2 * D_MODEL:col - 2 * D_MODEL + chunk] = (y * row_scale).astype(BF16)
            else:
                half_y = y * (0.5 * row_scale)
                sg_ref[:, col - 4 * D_MODEL:col - 4 * D_MODEL + chunk] = (
                    half_y + half_y * jnp.tanh(half_y)).astype(BF16)

        def project(c):
            return _dot(h, w_ref[:, c * chunk:(c + 1) * chunk])

        first_v, first_gate = 2 * D_MODEL // chunk, 4 * D_MODEL // chunk
        order = list(range(first_v)) + list(range(first_gate, n_chunks)) + list(range(first_v, first_gate))
        pending = project(order[0])
        for n, c in enumerate(order):
            following = project(order[n + 1]) if n + 1 < n_chunks else None
            epilogue(c, pending)
            pending = following


def _position_major(batch, seq, tile):
    tiles_per_seq = seq // tile

    def split(s):
        p, b = jnp.divmod(jnp.maximum(s - WEIGHT_CAST_STEPS, 0), batch)
        return p, b * tiles_per_seq + p

    return split


def _ret_proj(x2, g, g_layer, w, w_layer, cos, sin, log_g, batch, seq):
    t = x2.shape[0]
    half = R_QK_DIM // 2
    split = _position_major(batch, seq, ROW_TILE)
    rows = lambda cols: pl.BlockSpec((ROW_TILE, cols), lambda s: (split(s)[1], 0))
    table = pl.BlockSpec((ROW_TILE, half), lambda s: (split(s)[0], 0))
    return pl.pallas_call(
        _ret_proj_kernel,
        grid=(WEIGHT_CAST_STEPS + t // ROW_TILE,),
        in_specs=[rows(D_MODEL), _layer_row_spec(D_MODEL, g_layer), _slab_spec(w, w_layer),
                  table, table, _resident((R_HEADS, 1, 1))],
        out_specs=[rows(D_MODEL), rows(D_MODEL), rows(2 * D_MODEL), rows(2 * D_MODEL)],
        out_shape=[jax.ShapeDtypeStruct((t, D_MODEL), BF16), jax.ShapeDtypeStruct((t, D_MODEL), BF16),
                   jax.ShapeDtypeStruct((t, 2 * D_MODEL), BF16), jax.ShapeDtypeStruct((t, 2 * D_MODEL), BF16)],
        scratch_shapes=[pltpu.VMEM(w.shape[1:], BF16)],
        compiler_params=_params(),
        name="ret_proj",
    )(x2, g, w, cos, sin, log_g)


RET_HEADS_PER_STEP = 2


def _retention_kernel(q_ref, k_ref, v_ref, gch_ref, o_ref, *, seq, heads):
    c = RET_CHUNK
    n_chunks = seq // c
    ti = lax.broadcasted_iota(jnp.int32, (c, c), 0)
    ui = lax.broadcasted_iota(jnp.int32, (c, c), 1)
    causal = ui <= ti

    def chunk_matmuls(hh, n):
        rows = slice(n * c, (n + 1) * c)
        q = q_ref[rows, hh * R_QK_DIM:(hh + 1) * R_QK_DIM]
        k = k_ref[rows, hh * R_QK_DIM:(hh + 1) * R_QK_DIM]
        v = v_ref[rows, hh * R_V_DIM:(hh + 1) * R_V_DIM]
        return _dot_nt(q, k), _dot_tn(k, v)

    tasks = [(hh, n) for hh in range(heads) for n in range(n_chunks)]
    pending = chunk_matmuls(*tasks[0])
    decayed = None
    for t, (hh, n) in enumerate(tasks):
        rows = slice(n * c, (n + 1) * c)
        vcols = slice(hh * R_V_DIM, (hh + 1) * R_V_DIM)
        following = chunk_matmuls(*tasks[t + 1]) if t + 1 < len(tasks) else None
        s, update = pending
        y = _dot(jnp.where(causal, s, 0.0).astype(BF16), v_ref[rows, vcols])
        if n > 0:
            y = y + _dot(q_ref[rows, hh * R_QK_DIM:(hh + 1) * R_QK_DIM], decayed.astype(BF16))
        if n + 1 < n_chunks:
            decayed = (update if n == 0 else decayed + update) * gch_ref[hh]
        o_ref[rows, vcols] = (y * lax.rsqrt(jnp.mean(y * y, axis=-1, keepdims=True) + EPS)).astype(BF16)
        pending = following


def _retention(q, k, v, gch, batch, seq):
    hps = RET_HEADS_PER_STEP
    qk_spec = pl.BlockSpec((seq, hps * R_QK_DIM), lambda b, h: (b, h))
    v_spec = pl.BlockSpec((seq, hps * R_V_DIM), lambda b, h: (b, h))
    return pl.pallas_call(
        functools.partial(_retention_kernel, seq=seq, heads=hps),
        grid=(batch, R_HEADS // hps),
        in_specs=[qk_spec, qk_spec, v_spec, pl.BlockSpec((hps, 1, 1), lambda b, h: (h, 0, 0))],
        out_specs=v_spec,
        out_shape=jax.ShapeDtypeStruct(v.shape, BF16),
        compiler_params=pltpu.CompilerParams(dimension_semantics=("arbitrary", "arbitrary"),
                                             vmem_limit_bytes=VMEM_LIMIT),
        name="retention",
    )(q, k, v, gch)


def _rope_tables(seq, half):
    inv = ROPE_THETA ** (-jnp.arange(half, dtype=F32) / half)
    ang = jnp.arange(seq, dtype=jnp.int32).astype(F32)[:, None] * inv[None, :]
    return jnp.cos(ang), jnp.sin(ang)


def _retention_decay():
    log_g = jnp.log1p(-(2.0 ** (-5.0 - jnp.arange(R_HEADS, dtype=F32))))
    return log_g[:, None, None], jnp.exp(log_g * RET_CHUNK)[:, None, None]


def kernel(x, norm_mix_g, norm_mlp_g, a_w_qkv, a_q_gain, a_k_gain, a_w_o, r_w_in, r_w_out, mlp_w1, mlp_w2):
    batch, seq, d = x.shape
    assert d == D_MODEL and seq % ROW_TILE == 0 and seq % QKV_ROW_TILE == 0 and seq % MOBA_BLOCK == 0
    assert ROW_TILE % RET_CHUNK == 0
    depth = norm_mix_g.shape[0]
    x2 = x.reshape(batch * seq, d)
    norm_mix_g = norm_mix_g.reshape(depth, 1, d)
    norm_mlp_g = norm_mlp_g.reshape(depth, 1, d)

    cos_a, sin_a = _rope_tables(seq, A_HEAD_DIM // 2)
    cos_a = jnp.concatenate([cos_a, cos_a], axis=1)
    sin_a = jnp.concatenate([sin_a, sin_a], axis=1)
    cos_r, sin_r = _rope_tables(seq, R_QK_DIM // 2)
    log_g, gch = _retention_decay()

    a_q_gain = a_q_gain.reshape(-1, 1, A_HEAD_DIM)
    a_k_gain = a_k_gain.reshape(-1, 1, A_HEAD_DIM)

    for i in range(depth):
        j = i // 2
        if i % 2 == 0:
            q, k, v = _moba_qkv(x2, norm_mix_g, i, a_w_qkv, j, a_q_gain, a_k_gain, cos_a, sin_a, batch, seq)
            a, gate, w_o = _moba_attn(q, k, v, batch, seq), None, a_w_o
        else:
            q, k, v, gate = _ret_proj(x2, norm_mix_g, i, r_w_in, j, cos_r, sin_r, log_g, batch, seq)
            a, w_o = _retention(q, k, v, gch, batch, seq), r_w_out
        x2 = _mix_mlp(a, gate, w_o, j, x2, norm_mlp_g, mlp_w1, mlp_w2, i)
    return x2.reshape(batch, seq, d)
```
